```python
import jax, jax.numpy as jnp
from jax import lax
import numpy as np

D_MODEL = 1024
BATCH = 16
SEQ = 2048
DEPTH = 1
DEC_BATCH = 16
DEC_SEQ = 16
PAST_LEN = 1024

CHUNK = 64
EPS = 1e-6
A_CHUNK = 128
A_GROUPS = 8
A_WIDTH = D_MODEL
A_GDIM = A_WIDTH // A_GROUPS
B_HEADS = 4
B_DK = 256
B_DV = 512
B_QK = B_HEADS * B_DK
B_V = B_HEADS * B_DV
ROPE_BASE = 10000.0
SPLITS = (A_WIDTH, 2 * A_WIDTH, 2 * A_WIDTH + B_QK, 2 * A_WIDTH + 2 * B_QK,
          2 * A_WIDTH + 2 * B_QK + B_V, 2 * A_WIDTH + 2 * B_QK + 2 * B_V)
IN_COLS = 2 * A_WIDTH + 2 * B_QK + 2 * B_V + 2 * D_MODEL
P_HEADS = 8
P_NKEYS = 128
P_EXPERTS = P_NKEYS * P_NKEYS
P_DKEY = 256
P_HALF = P_DKEY // 2
P_TOPK = 16
P_BLOCK = 256

kernel_name = "hybrid_sgu_retention_peer_stream_step"


def rmsnorm(x, g):
    xf = x.astype(jnp.float32)
    y = xf * lax.rsqrt(jnp.mean(xf * xf, axis=-1, keepdims=True) + EPS)
    return (y * g.astype(jnp.float32)).astype(x.dtype)


def _rms(x):
    xf = x.astype(jnp.float32)
    return (xf * lax.rsqrt(jnp.mean(xf * xf, axis=-1, keepdims=True) + EPS)).astype(x.dtype)


def _log_gamma():
    return jnp.log(1.0 - 2.0 ** (-5.0 - jnp.arange(B_HEADS, dtype=jnp.float32)))


def _rotary(x, pos):
    half = B_DK // 2
    inv = 1.0 / (ROPE_BASE ** jnp.linspace(0.0, 1.0, half, dtype=jnp.float32))
    ang = pos.astype(jnp.float32)[:, None] * inv[None, :]
    cos, sin = jnp.cos(ang), jnp.sin(ang)
    x1 = x[..., :half].astype(jnp.float32)
    x2 = x[..., half:].astype(jnp.float32)
    return jnp.concatenate([x1 * cos - x2 * sin, x1 * sin + x2 * cos], axis=-1).astype(x.dtype)


def _retention_block(q, k, v, s, log_g):
    L = q.shape[2]
    dt = q.dtype
    idx = jnp.arange(L, dtype=jnp.float32)
    diff = idx[:, None] - idx[None, :]
    lg = log_g[:, None]
    decay = jnp.where(diff[None] >= 0.0, jnp.exp(lg[:, :, None] * jnp.maximum(diff, 0.0)[None]), 0.0).astype(dt)
    scores = jnp.einsum('bhnd,bhmd->bhnm', q, k) * decay
    out = jnp.einsum('bhnm,bhmv->bhnv', scores, v)
    q_dec = jnp.exp(lg * (idx + 1.0)).astype(dt)
    out = out + jnp.einsum('bhnd,bhdv->bhnv', q * q_dec[None, :, :, None], s)
    k_dec = jnp.exp(lg * (L - 1.0 - idx)).astype(dt)
    s_new = jnp.exp(log_g * L).astype(dt)[None, :, None, None] * s + jnp.einsum(
        'bhmd,bhmv->bhdv', k * k_dec[None, :, :, None], v)
    return out, s_new


def _mixer(h, pos0, s0, log_g, w_in, w_s, b_s, g_sgu, w_proj_a, w_proj_b, b_gate, w_out):
    B, L, _ = h.shape
    z = h @ w_in
    u_a, v_a, q, k, v, g, gate = jnp.split(z, SPLITS, axis=-1)
    u_a = jax.nn.gelu(u_a)
    v_a = rmsnorm(jax.nn.gelu(v_a), g_sgu)
    ac = min(L, A_CHUNK)
    nc = L // ac
    vr = v_a.reshape(B, nc, ac, A_GROUPS, A_GDIM)
    ws = w_s[:, :ac, :ac] * jnp.tril(jnp.ones((ac, ac), h.dtype))[None]
    mixed = jnp.einsum('gnm,bcmgd->bcngd', ws, vr) + b_s[:, :ac].T[None, None, :, :, None]
    y_a = u_a * mixed.reshape(B, L, A_WIDTH)
    pos = pos0 + jnp.arange(L)
    qh = _rotary(q.reshape(B, L, B_HEADS, B_DK).transpose(0, 2, 1, 3), pos)
    kh = _rotary(k.reshape(B, L, B_HEADS, B_DK).transpose(0, 2, 1, 3), pos) * (B_DK ** -0.5)
    vh = v.reshape(B, L, B_HEADS, B_DV).transpose(0, 2, 1, 3)
    c = min(L, CHUNK)
    ncb = L // c
    to_blocks = lambda t: t.reshape(B, B_HEADS, ncb, c, t.shape[-1]).transpose(2, 0, 1, 3, 4)

    def step(s, blk):
        qc, kc, vc = blk
        o, s = _retention_block(qc, kc, vc, s, log_g)
        return s, o

    s_new, o = lax.scan(step, s0.astype(h.dtype), (to_blocks(qh), to_blocks(kh), to_blocks(vh)))
    o = o.transpose(1, 2, 0, 3, 4).reshape(B, B_HEADS, L, B_DV)
    o = _rms(o).transpose(0, 2, 1, 3).reshape(B, L, B_V)
    y_b = jax.nn.silu(g) * o
    gates = jax.nn.sigmoid(gate + b_gate)
    g_a, g_b = jnp.split(gates, 2, axis=-1)
    m = g_a * (y_a @ w_proj_a) + g_b * (y_b @ w_proj_b)
    return m @ w_out, s_new, v_a


def _peer_block(t, w_query, k1, k2, eu, ev):
    T = t.shape[0]
    q = (t @ w_query).reshape(T, P_HEADS, P_DKEY)
    s1 = jnp.einsum('thd,nd->thn', q[..., :P_HALF], k1).astype(jnp.float32)
    s2 = jnp.einsum('thd,nd->thn', q[..., P_HALF:], k2).astype(jnp.float32)
    v1, i1 = lax.top_k(s1, P_TOPK)
    v2, i2 = lax.top_k(s2, P_TOPK)
    cand = (v1[..., :, None] + v2[..., None, :]).reshape(T, P_HEADS, P_TOPK * P_TOPK)
    vals, flat = lax.top_k(cand, P_TOPK)
    e1 = jnp.take_along_axis(i1, flat // P_TOPK, axis=-1)
    e2 = jnp.take_along_axis(i2, flat % P_TOPK, axis=-1)
    experts = e1 * P_NKEYS + e2
    w = jax.nn.softmax(vals, axis=-1).astype(t.dtype)
    a = jax.nn.gelu(jnp.einsum('td,thkd->thk', t, eu[experts]))
    return jnp.einsum('thk,thkd->td', w * a, ev[experts])


def _peer(h, w_query, k1, k2, eu, ev):
    B, L, D = h.shape
    n = B * L
    blk = min(P_BLOCK, n)
    pad = (-n) % blk
    t = jnp.pad(h.reshape(n, D), ((0, pad), (0, 0))).reshape(-1, blk, D)
    out = lax.map(lambda tb: _peer_block(tb, w_query, k1, k2, eu, ev), t)
    return out.reshape(-1, D)[:n].reshape(B, L, D)


def _layer(x, pos0, s0, log_g, w_in, w_s, b_s, g_sgu, w_proj_a, w_proj_b, b_gate, w_out,
           g_mix, g_ffn, w_query, k1, k2, eu, ev):
    m, s_new, v_a = _mixer(rmsnorm(x, g_mix), pos0, s0, log_g, w_in, w_s, b_s, g_sgu,
                           w_proj_a, w_proj_b, b_gate, w_out)
    x = x + m
    x = x + _peer(rmsnorm(x, g_ffn), w_query, k1, k2, eu, ev)
    return x, s_new, v_a


def setup_inputs(seed: int = 0) -> dict:
    key = jax.random.key(seed)
    ks = jax.random.split(key, 21)
    f = jnp.float32
    nrm = lambda k, shape, scale: jax.random.normal(k, shape, f) * scale
    return {
        "x_prompt": nrm(ks[0], (BATCH, SEQ, D_MODEL), 1.0),
        "x_sample": nrm(ks[1], (DEC_BATCH, DEC_SEQ, D_MODEL), 1.0),
        "state_ret": nrm(ks[2], (DEPTH, DEC_BATCH, B_HEADS, B_DK, B_DV), 0.1),
        "w_in": nrm(ks[3], (DEPTH, D_MODEL, IN_COLS), D_MODEL ** -0.5),
        "w_s": nrm(ks[4], (DEPTH, A_GROUPS, A_CHUNK, A_CHUNK), A_CHUNK ** -0.5),
        "b_s": 1.0 + nrm(ks[5], (DEPTH, A_GROUPS, A_CHUNK), 0.02),
        "g_sgu": 1.0 + nrm(ks[6], (DEPTH, A_WIDTH), 0.02),
        "w_proj_a": nrm(ks[7], (DEPTH, A_WIDTH, D_MODEL), A_WIDTH ** -0.5),
        "w_proj_b": nrm(ks[8], (DEPTH, B_V, D_MODEL), B_V ** -0.5),
        "b_gate": nrm(ks[9], (DEPTH, 2 * D_MODEL), 0.01),
        "w_out": nrm(ks[10], (DEPTH, D_MODEL, D_MODEL), D_MODEL ** -0.5),
        "g_mix": 1.0 + nrm(ks[11], (DEPTH, D_MODEL), 0.02),
        "g_ffn": 1.0 + nrm(ks[12], (DEPTH, D_MODEL), 0.02),
        "w_query": nrm(ks[13], (DEPTH, D_MODEL, P_HEADS * P_DKEY), D_MODEL ** -0.5),
        "sub_keys_1": nrm(ks[14], (DEPTH, P_NKEYS, P_HALF), P_HALF ** -0.5),
        "sub_keys_2": nrm(ks[15], (DEPTH, P_NKEYS, P_HALF), P_HALF ** -0.5),
        "expert_u": nrm(ks[16], (DEPTH, P_EXPERTS, D_MODEL), D_MODEL ** -0.5),
        "expert_v": nrm(ks[17], (DEPTH, P_EXPERTS, D_MODEL), D_MODEL ** -0.5),
        "g_final": 1.0 + nrm(ks[18], (D_MODEL,), 0.02),
    }


def reference(x_prompt, x_sample, state_ret, w_in, w_s, b_s, g_sgu, w_proj_a, w_proj_b, b_gate,
              w_out, g_mix, g_ffn, w_query, sub_keys_1, sub_keys_2, expert_u, expert_v, g_final):
    log_g = _log_gamma()
    xp, xs = x_prompt, x_sample
    sp_list, ss_list, v_list = [], [], []
    for l in range(DEPTH):
        p = (w_in[l], w_s[l], b_s[l], g_sgu[l], w_proj_a[l], w_proj_b[l], b_gate[l], w_out[l],
             g_mix[l], g_ffn[l], w_query[l], sub_keys_1[l], sub_keys_2[l], expert_u[l], expert_v[l])
        s0p = jnp.zeros((xp.shape[0], B_HEADS, B_DK, B_DV), xp.dtype)
        xp, sp, _ = _layer(xp, 0, s0p, log_g, *p)
        xs, ss, vs = _layer(xs, PAST_LEN, state_ret[l], log_g, *p)
        sp_list.append(sp)
        ss_list.append(ss)
        v_list.append(vs)
    y_prompt = rmsnorm(xp, g_final)
    y_sample = rmsnorm(xs, g_final)
    state_ret_prompt = jnp.stack(sp_list)
    state_ret_sample = jnp.stack(ss_list)
    sgu_v_sample = jnp.stack(v_list)
    return (y_prompt, y_sample, state_ret_prompt, state_ret_sample, sgu_v_sample)
```

```python
import functools

import jax
import jax.numpy as jnp
from jax import lax
from jax.experimental import pallas as pl
from jax.experimental.pallas import tpu as pltpu

F32 = jnp.float32
BF16 = jnp.bfloat16

D_MODEL = 1024
EPS = 1e-6
A_CHUNK = 128
A_GROUPS = 8
A_GDIM = D_MODEL // A_GROUPS
B_HEADS = 4
B_DK = 256
B_DV = 512
B_QK = B_HEADS * B_DK
B_V = B_HEADS * B_DV
ROPE_BASE = 10000.0
RET_CHUNK = 64
IN_COLS = 10 * D_MODEL
P_HEADS = 8
P_NKEYS = 128
P_HALF = 128
P_TOPK = 16
P_PAIRS = P_HEADS * P_TOPK

LANES = 128
TABLE_ROWS_PER_EXPERT = D_MODEL // (2 * LANES)
VMEM_LIMIT = 56 * 1024 * 1024


def _cparams(*sem):
    return pltpu.CompilerParams(dimension_semantics=sem, vmem_limit_bytes=VMEM_LIMIT)


def _rms(x):
    return x * lax.rsqrt(jnp.mean(x * x, axis=-1, keepdims=True) + EPS)


def _inproj_kernel(x_ref, gmix_ref, w_ref, gsgu_ref, bgate_ref, cos_ref, sin_ref, z_ref, vn_ref, h_scr):
    j = pl.program_id(1)

    @pl.when(j == 0)
    def _():
        h = _rms(x_ref[...]) * gmix_ref[...]
        h_scr[...] = h.astype(BF16)

    z = jnp.dot(h_scr[...], w_ref[...], preferred_element_type=F32)

    @pl.when(j == 0)
    def _():
        z_ref[...] = jax.nn.gelu(z).astype(BF16)

    @pl.when(j == 1)
    def _():
        vn = _rms(jax.nn.gelu(z)) * gsgu_ref[...]
        vn_ref[...] = vn
        z_ref[...] = vn.astype(BF16)

    def rotary(scale):
        cos = cos_ref[...]
        sin = sin_ref[...]
        half = B_DK // 2
        for h in range(B_HEADS):
            x1 = z[:, h * B_DK:h * B_DK + half]
            x2 = z[:, h * B_DK + half:(h + 1) * B_DK]
            z_ref[:, h * B_DK:h * B_DK + half] = ((x1 * cos - x2 * sin) * scale).astype(BF16)
            z_ref[:, h * B_DK + half:(h + 1) * B_DK] = ((x1 * sin + x2 * cos) * scale).astype(BF16)

    @pl.when(j == 2)
    def _():
        rotary(1.0)

    @pl.when(j == 3)
    def _():
        rotary(B_DK ** -0.5)

    @pl.when((j == 4) | (j == 5))
    def _():
        z_ref[...] = z.astype(BF16)

    @pl.when((j == 6) | (j == 7))
    def _():
        z_ref[...] = jax.nn.silu(z).astype(BF16)

    @pl.when(j == 8)
    def _():
        z_ref[...] = jax.nn.sigmoid(z + bgate_ref[:, :D_MODEL]).astype(BF16)

    @pl.when(j == 9)
    def _():
        z_ref[...] = jax.nn.sigmoid(z + bgate_ref[:, D_MODEL:]).astype(BF16)


def _inproj(x, g_mix, w_in_bf, g_sgu, b_gate, cos_t, sin_t, tm):
    n = x.shape[0]
    n_pos_blocks = cos_t.shape[0] // tm
    const = lambda i, j: (0, 0)
    return pl.pallas_call(
        _inproj_kernel,
        grid=(n // tm, IN_COLS // D_MODEL),
        in_specs=[
            pl.BlockSpec((tm, D_MODEL), lambda i, j: (i, 0)),
            pl.BlockSpec((1, D_MODEL), const),
            pl.BlockSpec((D_MODEL, D_MODEL), lambda i, j: (0, j)),
            pl.BlockSpec((1, D_MODEL), const),
            pl.BlockSpec((1, 2 * D_MODEL), const),
            pl.BlockSpec((tm, B_DK // 2), lambda i, j: (i % n_pos_blocks, 0)),
            pl.BlockSpec((tm, B_DK // 2), lambda i, j: (i % n_pos_blocks, 0)),
        ],
        out_specs=[
            pl.BlockSpec((tm, D_MODEL), lambda i, j: (i, j)),
            pl.BlockSpec((tm, D_MODEL), lambda i, j: (i, 0)),
        ],
        out_shape=[
            jax.ShapeDtypeStruct((n, IN_COLS), BF16),
            jax.ShapeDtypeStruct((n, D_MODEL), F32),
        ],
        scratch_shapes=[pltpu.VMEM((tm, D_MODEL), BF16)],
        compiler_params=_cparams("parallel", "arbitrary"),
        name="inproj",
    )(x, g_mix.reshape(1, -1), w_in_bf, g_sgu.reshape(1, -1), b_gate.reshape(1, -1), cos_t, sin_t)


def _sgu_kernel(ac, u_ref, v_ref, wt_ref, bias_ref, wp_ref, pa_ref, wm_scr):
    r = u_ref.shape[0]

    @pl.when(pl.program_id(0) == 0)
    def _():
        row = lax.broadcasted_iota(jnp.int32, (r, r), 0)
        col = lax.broadcasted_iota(jnp.int32, (r, r), 1)
        keep = (row // ac == col // ac) & (col <= row)
        for g in range(A_GROUPS):
            wm_scr[g] = jnp.where(keep, wt_ref[g], 0.0).astype(BF16)

    v = v_ref[...]
    mixed = jnp.concatenate(
        [jnp.dot(wm_scr[g], v[:, g * A_GDIM:(g + 1) * A_GDIM], preferred_element_type=F32)
         for g in range(A_GROUPS)], axis=1) + bias_ref[...]
    y = (u_ref[...].astype(F32) * mixed).astype(BF16)
    pa_ref[...] = jnp.dot(y, wp_ref[...], preferred_element_type=F32).astype(BF16)


def _sgu(z_act, w_tiled, bias_full, w_proj_a_bf, ac, r):
    n = z_act.shape[0]
    return pl.pallas_call(
        functools.partial(_sgu_kernel, ac),
        grid=(n // r,),
        in_specs=[
            pl.BlockSpec((r, D_MODEL), lambda i: (i, 0)),
            pl.BlockSpec((r, D_MODEL), lambda i: (i, 1)),
            pl.BlockSpec((A_GROUPS, r, r), lambda i: (0, 0, 0)),
            pl.BlockSpec((r, D_MODEL), lambda i: (0, 0)),
            pl.BlockSpec((D_MODEL, D_MODEL), lambda i: (0, 0)),
        ],
        out_specs=pl.BlockSpec((r, D_MODEL), lambda i: (i, 0)),
        out_shape=jax.ShapeDtypeStruct((n, D_MODEL), BF16),
        scratch_shapes=[pltpu.VMEM((A_GROUPS, r, r), BF16)],
        compiler_params=_cparams("arbitrary"),
        name="sgu",
    )(z_act, z_act, w_tiled, bias_full, w_proj_a_bf)


def _ret_kernel(q_ref, k_ref, v_ref, g_ref, s0_ref, dmask_ref, qdec_ref, kdec_ref, gc_ref,
                wp_ref, pb_ref, sout_ref, s_scr):
    c = pl.program_id(1)

    @pl.when(c == 0)
    def _():
        s_scr[...] = s0_ref[0]

    acc = None
    for h in range(B_HEADS):
        q = q_ref[:, h * B_DK:(h + 1) * B_DK]
        k = k_ref[:, h * B_DK:(h + 1) * B_DK]
        v = v_ref[:, h * B_DV:(h + 1) * B_DV]
        s = s_scr[h]
        scores = lax.dot_general(q, k, (((1,), (1,)), ((), ())), preferred_element_type=F32) * dmask_ref[h]
        o = jnp.dot(scores.astype(BF16), v, preferred_element_type=F32)
        qd = (q.astype(F32) * qdec_ref[h]).astype(BF16)
        o = o + jnp.dot(qd, s.astype(BF16), preferred_element_type=F32)
        kd = (k.astype(F32) * kdec_ref[h]).astype(BF16)
        s_scr[h] = gc_ref[h] * s + lax.dot_general(kd, v, (((0,), (0,)), ((), ())), preferred_element_type=F32)
        y = (g_ref[:, h * B_DV:(h + 1) * B_DV].astype(F32) * _rms(o)).astype(BF16)
        part = jnp.dot(y, wp_ref[h * B_DV:(h + 1) * B_DV, :], preferred_element_type=F32)
        acc = part if acc is None else acc + part
    pb_ref[...] = acc.astype(BF16)

    @pl.when(c == pl.num_programs(1) - 1)
    def _():
        sout_ref[0] = s_scr[...]


def _ret(z_act, s0, w_proj_b_bf, n_seq, seq_len, chunk):
    n = z_act.shape[0]
    nc = seq_len // chunk
    dmask, qdec, kdec, gc = _ret_consts(chunk)
    row = lambda b, c: b * nc + c
    state_spec = pl.BlockSpec((1, B_HEADS, B_DK, B_DV), lambda b, c: (b, 0, 0, 0))
    full = lambda shape: pl.BlockSpec(shape, lambda b, c: (0,) * len(shape))
    return pl.pallas_call(
        _ret_kernel,
        grid=(n_seq, nc),
        in_specs=[
            pl.BlockSpec((chunk, B_QK), lambda b, c: (row(b, c), 2)),
            pl.BlockSpec((chunk, B_QK), lambda b, c: (row(b, c), 3)),
            pl.BlockSpec((chunk, B_V), lambda b, c: (row(b, c), 2)),
            pl.BlockSpec((chunk, B_V), lambda b, c: (row(b, c), 3)),
            state_spec,
            full((B_HEADS, chunk, chunk)),
            full((B_HEADS, chunk, 1)),
            full((B_HEADS, chunk, 1)),
            full((B_HEADS, 1, 1)),
            full((B_V, D_MODEL)),
        ],
        out_specs=[
            pl.BlockSpec((chunk, D_MODEL), lambda b, c: (row(b, c), 0)),
            state_spec,
        ],
        out_shape=[
            jax.ShapeDtypeStruct((n, D_MODEL), BF16),
            jax.ShapeDtypeStruct((n_seq, B_HEADS, B_DK, B_DV), F32),
        ],
        scratch_shapes=[pltpu.VMEM((B_HEADS, B_DK, B_DV), F32)],
        compiler_params=_cparams("parallel", "arbitrary"),
        name="ret",
    )(z_act, z_act, z_act, z_act, s0, dmask, qdec, kdec, gc, w_proj_b_bf)


def _ret_consts(chunk):
    log_g = jnp.log(1.0 - 2.0 ** (-5.0 - jnp.arange(B_HEADS, dtype=F32)))
    idx = jnp.arange(chunk, dtype=F32)
    diff = idx[:, None] - idx[None, :]
    lg = log_g[:, None]
    dmask = jnp.where(diff[None] >= 0.0, jnp.exp(lg[:, :, None] * jnp.maximum(diff, 0.0)[None]), 0.0)
    qdec = jnp.exp(lg * (idx + 1.0))[:, :, None]
    kdec = jnp.exp(lg * (chunk - 1.0 - idx))[:, :, None]
    gc = jnp.exp(log_g * chunk)[:, None, None]
    return dmask, qdec, kdec, gc


def _merge_kernel(ga_ref, gb_ref, pa_ref, pb_ref, x_ref, wo_ref, gffn_ref, wq_ref, k1_ref, k2_ref,
                  x1_ref, t_ref, s1_ref, s2_ref):
    m = (ga_ref[...].astype(F32) * pa_ref[...].astype(F32)
         + gb_ref[...].astype(F32) * pb_ref[...].astype(F32))
    x1 = x_ref[...] + jnp.dot(m.astype(BF16), wo_ref[...], preferred_element_type=F32)
    x1_ref[...] = x1
    t = _rms(x1) * gffn_ref[...]
    t_ref[...] = t
    qq = jnp.dot(t.astype(BF16), wq_ref[...], preferred_element_type=F32).astype(BF16)
    nt = (((1,), (1,)), ((), ()))
    for h in range(P_HEADS):
        base = h * 2 * P_HALF
        s1_ref[h] = lax.dot_general(k1_ref[...], qq[:, base:base + P_HALF], nt, preferred_element_type=F32)
        s2_ref[h] = lax.dot_general(k2_ref[...], qq[:, base + P_HALF:base + 2 * P_HALF], nt,
                                    preferred_element_type=F32)


def _merge(z_act, pa, pb, x, w_out_bf, g_ffn, w_query_bf, k1_bf, k2_bf, tm):
    n = x.shape[0]
    tok = lambda i: (i, 0)
    const = lambda i: (0, 0)
    return pl.pallas_call(
        _merge_kernel,
        grid=(n // tm,),
        in_specs=[
            pl.BlockSpec((tm, D_MODEL), lambda i: (i, 8)),
            pl.BlockSpec((tm, D_MODEL), lambda i: (i, 9)),
            pl.BlockSpec((tm, D_MODEL), tok),
            pl.BlockSpec((tm, D_MODEL), tok),
            pl.BlockSpec((tm, D_MODEL), tok),
            pl.BlockSpec((D_MODEL, D_MODEL), const),
            pl.BlockSpec((1, D_MODEL), const),
            pl.BlockSpec((D_MODEL, 2 * P_HALF * P_HEADS), const),
            pl.BlockSpec((P_NKEYS, P_HALF), const),
            pl.BlockSpec((P_NKEYS, P_HALF), const),
        ],
        out_specs=[
            pl.BlockSpec((tm, D_MODEL), tok),
            pl.BlockSpec((tm, D_MODEL), tok),
            pl.BlockSpec((P_HEADS, P_NKEYS, tm), lambda i: (0, 0, i)),
            pl.BlockSpec((P_HEADS, P_NKEYS, tm), lambda i: (0, 0, i)),
        ],
        out_shape=[
            jax.ShapeDtypeStruct((n, D_MODEL), F32),
            jax.ShapeDtypeStruct((n, D_MODEL), F32),
            jax.ShapeDtypeStruct((P_HEADS, P_NKEYS, n), F32),
            jax.ShapeDtypeStruct((P_HEADS, P_NKEYS, n), F32),
        ],
        compiler_params=_cparams("parallel"),
        name="merge",
    )(z_act, z_act, pa, pb, x, w_out_bf, g_ffn.reshape(1, -1), w_query_bf, k1_bf, k2_bf)


def _extract_top(vals, ids, count, big_id):
    out_v, out_i = [], []
    for _ in range(count):
        m = jnp.max(vals, axis=0, keepdims=True)
        sel = jnp.min(jnp.where(vals == m, ids, big_id), axis=0, keepdims=True)
        out_v.append(m)
        out_i.append(sel)
        vals = jnp.where(ids == sel, -jnp.inf, vals)
    return out_v, out_i


def _topk_kernel(s1_ref, s2_ref, idx_ref, w_ref):
    tl = s1_ref.shape[2]
    key_id = lax.broadcasted_iota(jnp.int32, (P_NKEYS, tl), 0).astype(F32)
    sub8 = lax.broadcasted_iota(jnp.int32, (8, tl), 0).astype(F32)
    sub16 = lax.broadcasted_iota(jnp.int32, (P_TOPK, tl), 0).astype(F32)

    def head_body(h, carry):
        v1, i1 = _extract_top(s1_ref[h], key_id, P_TOPK, P_NKEYS)
        v2, i2 = _extract_top(s2_ref[h], key_id, P_TOPK, P_NKEYS)
        v2_all = jnp.concatenate(v2, axis=0)
        i2_all = jnp.concatenate(i2, axis=0)
        cand_v, cand_flat, cand_e = [], [], []
        for a in range(P_TOPK):
            nb = P_TOPK // (a + 1)
            if nb > 8:
                sub, v2a, i2a = sub16, v2_all, i2_all
            else:
                sub, v2a, i2a = sub8, v2_all[:8], i2_all[:8]
            ok = sub < nb
            cand_v.append(jnp.where(ok, v1[a] + v2a, -jnp.inf))
            cand_flat.append(a * P_TOPK + sub)
            cand_e.append(i1[a] * P_NKEYS + i2a)
        cv = jnp.concatenate(cand_v, axis=0)
        cf = jnp.concatenate(cand_flat, axis=0)
        ce = jnp.concatenate(cand_e, axis=0)
        big = P_TOPK * P_TOPK
        vals, experts = [], []
        for _ in range(P_TOPK):
            m = jnp.max(cv, axis=0, keepdims=True)
            sel = jnp.min(jnp.where(cv == m, cf, big), axis=0, keepdims=True)
            hit = cf == sel
            vals.append(m)
            experts.append(jnp.max(jnp.where(hit, ce, -1.0), axis=0, keepdims=True))
            cv = jnp.where(hit, -jnp.inf, cv)
        vals = jnp.concatenate(vals, axis=0)
        e = jnp.exp(vals - vals[0:1])
        w = e / jnp.sum(e, axis=0, keepdims=True)
        idx_ref[h] = jnp.concatenate(experts, axis=0).astype(jnp.int32) * TABLE_ROWS_PER_EXPERT
        w_ref[h] = w
        return carry

    lax.fori_loop(0, P_HEADS, head_body, 0)


def _topk(s1t, s2t, tl):
    n = s1t.shape[2]
    sc = pl.BlockSpec((P_HEADS, P_NKEYS, tl), lambda i: (0, 0, i))
    out = pl.BlockSpec((P_HEADS, P_TOPK, tl), lambda i: (0, 0, i))
    idx, w = pl.pallas_call(
        _topk_kernel,
        grid=(n // tl,),
        in_specs=[sc, sc],
        out_specs=[out, out],
        out_shape=[
            jax.ShapeDtypeStruct((P_HEADS, P_TOPK, n), jnp.int32),
            jax.ShapeDtypeStruct((P_HEADS, P_TOPK, n), F32),
        ],
        compiler_params=_cparams("parallel"),
        name="topk",
    )(s1t, s2t)
    return idx.reshape(P_PAIRS, n), w.reshape(P_PAIRS, n)


def _unpack_rows(x):
    lo = lax.bitcast_convert_type(x << 16, F32)
    hi = lax.bitcast_convert_type(x & jnp.uint32(0xFFFF0000), F32)
    return lo, hi


def _peer_u_kernel(idx_ref, t_ref, w_ref, tab_ref, c_ref, q_scr):
    tb = t_ref.shape[0]
    rpe = TABLE_ROWS_PER_EXPERT
    lane = lax.broadcasted_iota(jnp.int32, (P_PAIRS, tb), 1)

    def token_body(t, acc):
        tt = t_ref[t]
        t_lo = tt[:rpe]
        t_hi = tt[rpe:]
        for k in range(P_PAIRS):
            row = pl.multiple_of(idx_ref[k, t], rpe)
            lo, hi = _unpack_rows(tab_ref[pl.ds(row, rpe), :])
            q_scr[k * rpe:(k + 1) * rpe, :] = lo * t_lo + hi * t_hi
        part = q_scr[pl.ds(0, P_PAIRS, stride=rpe), :]
        for r in range(1, rpe):
            part = part + q_scr[pl.ds(r, P_PAIRS, stride=rpe), :]
        col = jnp.sum(part, axis=1, keepdims=True)
        return jnp.where(lane == t, col, acc)

    a = lax.fori_loop(0, tb, token_body, jnp.zeros((P_PAIRS, tb), F32))
    c_ref[...] = w_ref[...] * jax.nn.gelu(a)


def _peer_u(idx_t, t3, w_t, table, tb):
    n = t3.shape[0]
    return pl.pallas_call(
        _peer_u_kernel,
        grid=(n // tb,),
        in_specs=[
            pl.BlockSpec((P_PAIRS, tb), lambda i: (0, i), memory_space=pltpu.SMEM),
            pl.BlockSpec((tb, 8, LANES), lambda i: (i, 0, 0)),
            pl.BlockSpec((P_PAIRS, tb), lambda i: (0, i)),
            pl.BlockSpec(table.shape, lambda i: (0, 0), pipeline_mode=pl.Buffered(1)),
        ],
        out_specs=pl.BlockSpec((P_PAIRS, tb), lambda i: (0, i)),
        out_shape=jax.ShapeDtypeStruct((P_PAIRS, n), F32),
        scratch_shapes=[pltpu.VMEM((P_PAIRS * TABLE_ROWS_PER_EXPERT, LANES), F32)],
        compiler_params=_cparams("arbitrary"),
        name="peer_u",
    )(idx_t, t3, w_t, table)


def _peer_v_kernel(idx_ref, c_ref, x1_ref, gfin_ref, tab_ref, y_ref):
    tb = x1_ref.shape[0]
    rpe = TABLE_ROWS_PER_EXPERT
    n_acc = 2

    def token_body(t, carry):
        acc_lo = [jnp.zeros((rpe, LANES), F32) for _ in range(n_acc)]
        acc_hi = [jnp.zeros((rpe, LANES), F32) for _ in range(n_acc)]
        for k in range(P_PAIRS):
            row = pl.multiple_of(idx_ref[k, t], rpe)
            coef = c_ref[k, t]
            lo, hi = _unpack_rows(tab_ref[pl.ds(row, rpe), :])
            acc_lo[k % n_acc] = acc_lo[k % n_acc] + coef * lo
            acc_hi[k % n_acc] = acc_hi[k % n_acc] + coef * hi
        peer = jnp.concatenate([sum(acc_lo[1:], acc_lo[0]), sum(acc_hi[1:], acc_hi[0])], axis=0)
        x2 = x1_ref[t] + peer
        ms = jnp.sum(jnp.sum(x2 * x2, axis=1, keepdims=True), axis=0, keepdims=True) / D_MODEL
        y_ref[t] = x2 * lax.rsqrt(ms + EPS) * gfin_ref[...]
        return carry

    lax.fori_loop(0, tb, token_body, 0)


def _peer_v(idx_t, c_t, x1_3, g_final, table, tb):
    n = x1_3.shape[0]
    smem = pl.BlockSpec((P_PAIRS, tb), lambda i: (0, i), memory_space=pltpu.SMEM)
    tok = pl.BlockSpec((tb, 8, LANES), lambda i: (i, 0, 0))
    return pl.pallas_call(
        _peer_v_kernel,
        grid=(n // tb,),
        in_specs=[
            smem, smem, tok,
            pl.BlockSpec((8, LANES), lambda i: (0, 0)),
            pl.BlockSpec(table.shape, lambda i: (0, 0), pipeline_mode=pl.Buffered(1)),
        ],
        out_specs=tok,
        out_shape=jax.ShapeDtypeStruct((n, 8, LANES), F32),
        compiler_params=_cparams("arbitrary"),
        name="peer_v",
    )(idx_t, c_t, x1_3, g_final.reshape(8, LANES), table)


def _pack_table(e):
    bits = lax.bitcast_convert_type(e.astype(BF16), jnp.uint16).astype(jnp.uint32)
    half = D_MODEL // 2
    words = bits[:, :half] | (bits[:, half:] << 16)
    return words.reshape(e.shape[0] * TABLE_ROWS_PER_EXPERT, LANES)


def _rope_tables(pos):
    half = B_DK // 2
    inv = 1.0 / (ROPE_BASE ** jnp.linspace(0.0, 1.0, half, dtype=F32))
    ang = pos.astype(F32)[:, None] * inv[None, :]
    return jnp.cos(ang), jnp.sin(ang)


def _layer(x, n_seq, seq_len, pos0, s0, p, tabs, g_final):
    n = x.shape[0]
    tm_in = min(n, 1024)
    rows = 256
    ac = min(seq_len, A_CHUNK)
    chunk = min(seq_len, 256)
    assert seq_len % chunk == 0 and rows % ac == 0 and n % rows == 0 and n % tm_in == 0
    assert seq_len <= RET_CHUNK or seq_len % RET_CHUNK == 0

    pos = pos0 + jnp.arange(seq_len)
    if seq_len < tm_in:
        pos = jnp.tile(pos, tm_in // seq_len)
    cos_t, sin_t = _rope_tables(pos)
    z_act, v_norm = _inproj(x, p["g_mix"], p["w_in"], p["g_sgu"], p["b_gate"], cos_t, sin_t, tm_in)

    w_tiled = jnp.tile(p["w_s"][:, :ac, :ac], (1, rows // ac, rows // ac))
    bias_full = jnp.repeat(jnp.tile(p["b_s"][:, :ac].T, (rows // ac, 1)), A_GDIM, axis=1)
    pa = _sgu(z_act, w_tiled, bias_full, p["w_proj_a"], ac, rows)
    pb, s_new = _ret(z_act, s0, p["w_proj_b"], n_seq, seq_len, chunk)

    x1, t, s1t, s2t = _merge(z_act, pa, pb, x, p["w_out"], p["g_ffn"], p["w_query"], p["k1"], p["k2"], rows)
    idx_t, w_t = _topk(s1t, s2t, LANES)
    c_t = _peer_u(idx_t, t.reshape(n, 8, LANES), w_t, tabs[0], LANES)
    y = _peer_v(idx_t, c_t, x1.reshape(n, 8, LANES), g_final, tabs[1], LANES)
    return y.reshape(n, D_MODEL), s_new, v_norm


def kernel(x_prompt, x_sample, state_ret, w_in, w_s, b_s, g_sgu, w_proj_a, w_proj_b, b_gate, w_out, g_mix,
           g_ffn, w_query, sub_keys_1, sub_keys_2, expert_u, expert_v, g_final):
    assert w_in.shape[0] == 1, "single-layer trunk only"
    bp, lp, _ = x_prompt.shape
    bs, ls, _ = x_sample.shape
    past_len = 1024
    p = dict(
        g_mix=g_mix[0], w_in=w_in[0].astype(BF16), g_sgu=g_sgu[0], b_gate=b_gate[0],
        w_s=w_s[0], b_s=b_s[0], w_proj_a=w_proj_a[0].astype(BF16), w_proj_b=w_proj_b[0].astype(BF16),
        w_out=w_out[0].astype(BF16), g_ffn=g_ffn[0], w_query=w_query[0].astype(BF16),
        k1=sub_keys_1[0].astype(BF16), k2=sub_keys_2[0].astype(BF16))
    tabs = (_pack_table(expert_u[0]), _pack_table(expert_v[0]))
    s0p = jnp.zeros((bp, B_HEADS, B_DK, B_DV), F32)
    yp, sp, _ = _layer(x_prompt.reshape(bp * lp, D_MODEL), bp, lp, 0, s0p, p, tabs, g_final)
    ys, ss, vs = _layer(x_sample.reshape(bs * ls, D_MODEL), bs, ls, past_len, state_ret[0], p, tabs, g_final)
    return (yp.reshape(bp, lp, D_MODEL), ys.reshape(bs, ls, D_MODEL), sp[None], ss[None],
            vs.reshape(1, bs, ls, D_MODEL))
```

```python
import functools

import jax
import jax.numpy as jnp
from jax import lax
from jax.experimental import pallas as pl
from jax.experimental.pallas import tpu as pltpu

F32 = jnp.float32
BF16 = jnp.bfloat16

D_MODEL = 1024
EPS = 1e-6
A_CHUNK = 128
A_GROUPS = 8
A_GDIM = D_MODEL // A_GROUPS
B_HEADS = 4
B_DK = 256
B_DV = 512
B_QK = B_HEADS * B_DK
B_V = B_HEADS * B_DV
ROPE_BASE = 10000.0
RET_CHUNK = 64
IN_COLS = 10 * D_MODEL
P_HEADS = 8
P_NKEYS = 128
P_HALF = 128
P_TOPK = 16
P_PAIRS = P_HEADS * P_TOPK

LANES = 128
TABLE_ROWS_PER_EXPERT = D_MODEL // (2 * LANES)
PEER_TOKENS_PER_ITER = 8
VMEM_LIMIT = 56 * 1024 * 1024


def _cparams(*sem):
    return pltpu.CompilerParams(dimension_semantics=sem, vmem_limit_bytes=VMEM_LIMIT)


def _rms(x):
    return x * lax.rsqrt(jnp.mean(x * x, axis=-1, keepdims=True) + EPS)


def _inproj_kernel(x_ref, gmix_ref, w_ref, gsgu_ref, bgate_ref, cos_ref, sin_ref, z_ref, vn_ref, h_scr):
    j = pl.program_id(1)

    @pl.when(j == 0)
    def _():
        h = _rms(x_ref[...]) * gmix_ref[...]
        h_scr[...] = h.astype(BF16)

    z = jnp.dot(h_scr[...], w_ref[...], preferred_element_type=F32)

    @pl.when(j == 0)
    def _():
        z_ref[...] = jax.nn.gelu(z).astype(BF16)

    @pl.when(j == 1)
    def _():
        vn = _rms(jax.nn.gelu(z)) * gsgu_ref[...]
        vn_ref[...] = vn
        z_ref[...] = vn.astype(BF16)

    def rotary(scale):
        cos = cos_ref[...]
        sin = sin_ref[...]
        half = B_DK // 2
        for h in range(B_HEADS):
            x1 = z[:, h * B_DK:h * B_DK + half]
            x2 = z[:, h * B_DK + half:(h + 1) * B_DK]
            z_ref[:, h * B_DK:h * B_DK + half] = ((x1 * cos - x2 * sin) * scale).astype(BF16)
            z_ref[:, h * B_DK + half:(h + 1) * B_DK] = ((x1 * sin + x2 * cos) * scale).astype(BF16)

    @pl.when(j == 2)
    def _():
        rotary(1.0)

    @pl.when(j == 3)
    def _():
        rotary(B_DK ** -0.5)

    @pl.when((j == 4) | (j == 5))
    def _():
        z_ref[...] = z.astype(BF16)

    @pl.when((j == 6) | (j == 7))
    def _():
        z_ref[...] = jax.nn.silu(z).astype(BF16)

    @pl.when(j == 8)
    def _():
        z_ref[...] = jax.nn.sigmoid(z + bgate_ref[:, :D_MODEL]).astype(BF16)

    @pl.when(j == 9)
    def _():
        z_ref[...] = jax.nn.sigmoid(z + bgate_ref[:, D_MODEL:]).astype(BF16)


def _inproj(x, g_mix, w_in_bf, g_sgu, b_gate, cos_t, sin_t, tm):
    n = x.shape[0]
    n_pos_blocks = cos_t.shape[0] // tm
    const = lambda i, j: (0, 0)
    return pl.pallas_call(
        _inproj_kernel,
        grid=(n // tm, IN_COLS // D_MODEL),
        in_specs=[
            pl.BlockSpec((tm, D_MODEL), lambda i, j: (i, 0)),
            pl.BlockSpec((1, D_MODEL), const),
            pl.BlockSpec((D_MODEL, D_MODEL), lambda i, j: (0, j)),
            pl.BlockSpec((1, D_MODEL), const),
            pl.BlockSpec((1, 2 * D_MODEL), const),
            pl.BlockSpec((tm, B_DK // 2), lambda i, j: (i % n_pos_blocks, 0)),
            pl.BlockSpec((tm, B_DK // 2), lambda i, j: (i % n_pos_blocks, 0)),
        ],
        out_specs=[
            pl.BlockSpec((tm, D_MODEL), lambda i, j: (i, j)),
            pl.BlockSpec((tm, D_MODEL), lambda i, j: (i, 0)),
        ],
        out_shape=[
            jax.ShapeDtypeStruct((n, IN_COLS), BF16),
            jax.ShapeDtypeStruct((n, D_MODEL), F32),
        ],
        scratch_shapes=[pltpu.VMEM((tm, D_MODEL), BF16)],
        compiler_params=_cparams("parallel", "arbitrary"),
        name="inproj",
    )(x, g_mix.reshape(1, -1), w_in_bf, g_sgu.reshape(1, -1), b_gate.reshape(1, -1), cos_t, sin_t)


def _sgu_kernel(ac, u_ref, v_ref, wt_ref, bias_ref, wp_ref, pa_ref, wm_scr):
    r = u_ref.shape[0]

    @pl.when(pl.program_id(0) == 0)
    def _():
        row = lax.broadcasted_iota(jnp.int32, (r, r), 0)
        col = lax.broadcasted_iota(jnp.int32, (r, r), 1)
        keep = (row // ac == col // ac) & (col <= row)
        for g in range(A_GROUPS):
            wm_scr[g] = jnp.where(keep, wt_ref[g], 0.0).astype(BF16)

    v = v_ref[...]
    mixed = jnp.concatenate(
        [jnp.dot(wm_scr[g], v[:, g * A_GDIM:(g + 1) * A_GDIM], preferred_element_type=F32)
         for g in range(A_GROUPS)], axis=1) + bias_ref[...]
    y = (u_ref[...].astype(F32) * mixed).astype(BF16)
    pa_ref[...] = jnp.dot(y, wp_ref[...], preferred_element_type=F32).astype(BF16)


def _sgu(z_act, w_tiled, bias_full, w_proj_a_bf, ac, r):
    n = z_act.shape[0]
    return pl.pallas_call(
        functools.partial(_sgu_kernel, ac),
        grid=(n // r,),
        in_specs=[
            pl.BlockSpec((r, D_MODEL), lambda i: (i, 0)),
            pl.BlockSpec((r, D_MODEL), lambda i: (i, 1)),
            pl.BlockSpec((A_GROUPS, r, r), lambda i: (0, 0, 0)),
            pl.BlockSpec((r, D_MODEL), lambda i: (0, 0)),
            pl.BlockSpec((D_MODEL, D_MODEL), lambda i: (0, 0)),
        ],
        out_specs=pl.BlockSpec((r, D_MODEL), lambda i: (i, 0)),
        out_shape=jax.ShapeDtypeStruct((n, D_MODEL), BF16),
        scratch_shapes=[pltpu.VMEM((A_GROUPS, r, r), BF16)],
        compiler_params=_cparams("arbitrary"),
        name="sgu",
    )(z_act, z_act, w_tiled, bias_full, w_proj_a_bf)


def _ret_kernel(q_ref, k_ref, v_ref, g_ref, s0_ref, dmask_ref, qdec_ref, kdec_ref, gc_ref,
                wp_ref, pb_ref, sout_ref, s_scr):
    c = pl.program_id(1)

    @pl.when(c == 0)
    def _():
        s_scr[...] = s0_ref[0]

    acc = None
    for h in range(B_HEADS):
        q = q_ref[:, h * B_DK:(h + 1) * B_DK]
        k = k_ref[:, h * B_DK:(h + 1) * B_DK]
        v = v_ref[:, h * B_DV:(h + 1) * B_DV]
        s = s_scr[h]
        scores = lax.dot_general(q, k, (((1,), (1,)), ((), ())), preferred_element_type=F32) * dmask_ref[h]
        o = jnp.dot(scores.astype(BF16), v, preferred_element_type=F32)
        qd = (q.astype(F32) * qdec_ref[h]).astype(BF16)
        o = o + jnp.dot(qd, s.astype(BF16), preferred_element_type=F32)
        kd = (k.astype(F32) * kdec_ref[h]).astype(BF16)
        s_scr[h] = gc_ref[h] * s + lax.dot_general(kd, v, (((0,), (0,)), ((), ())), preferred_element_type=F32)
        y = (g_ref[:, h * B_DV:(h + 1) * B_DV].astype(F32) * _rms(o)).astype(BF16)
        part = jnp.dot(y, wp_ref[h * B_DV:(h + 1) * B_DV, :], preferred_element_type=F32)
        acc = part if acc is None else acc + part
    pb_ref[...] = acc.astype(BF16)

    @pl.when(c == pl.num_programs(1) - 1)
    def _():
        sout_ref[0] = s_scr[...]


def _ret(z_act, s0, w_proj_b_bf, n_seq, seq_len, chunk):
    n = z_act.shape[0]
    nc = seq_len // chunk
    dmask, qdec, kdec, gc = _ret_consts(chunk)
    row = lambda b, c: b * nc + c
    state_spec = pl.BlockSpec((1, B_HEADS, B_DK, B_DV), lambda b, c: (b, 0, 0, 0))
    full = lambda shape: pl.BlockSpec(shape, lambda b, c: (0,) * len(shape))
    return pl.pallas_call(
        _ret_kernel,
        grid=(n_seq, nc),
        in_specs=[
            pl.BlockSpec((chunk, B_QK), lambda b, c: (row(b, c), 2)),
            pl.BlockSpec((chunk, B_QK), lambda b, c: (row(b, c), 3)),
            pl.BlockSpec((chunk, B_V), lambda b, c: (row(b, c), 2)),
            pl.BlockSpec((chunk, B_V), lambda b, c: (row(b, c), 3)),
            state_spec,
            full((B_HEADS, chunk, chunk)),
            full((B_HEADS, chunk, 1)),
            full((B_HEADS, chunk, 1)),
            full((B_HEADS, 1, 1)),
            full((B_V, D_MODEL)),
        ],
        out_specs=[
            pl.BlockSpec((chunk, D_MODEL), lambda b, c: (row(b, c), 0)),
            state_spec,
        ],
        out_shape=[
            jax.ShapeDtypeStruct((n, D_MODEL), BF16),
            jax.ShapeDtypeStruct((n_seq, B_HEADS, B_DK, B_DV), F32),
        ],
        scratch_shapes=[pltpu.VMEM((B_HEADS, B_DK, B_DV), F32)],
        compiler_params=_cparams("parallel", "arbitrary"),
        name="ret",
    )(z_act, z_act, z_act, z_act, s0, dmask, qdec, kdec, gc, w_proj_b_bf)


def _ret_consts(chunk):
    log_g = jnp.log(1.0 - 2.0 ** (-5.0 - jnp.arange(B_HEADS, dtype=F32)))
    idx = jnp.arange(chunk, dtype=F32)
    diff = idx[:, None] - idx[None, :]
    lg = log_g[:, None]
    dmask = jnp.where(diff[None] >= 0.0, jnp.exp(lg[:, :, None] * jnp.maximum(diff, 0.0)[None]), 0.0)
    qdec = jnp.exp(lg * (idx + 1.0))[:, :, None]
    kdec = jnp.exp(lg * (chunk - 1.0 - idx))[:, :, None]
    gc = jnp.exp(log_g * chunk)[:, None, None]
    return dmask, qdec, kdec, gc


def _merge_kernel(ga_ref, gb_ref, pa_ref, pb_ref, x_ref, wo_ref, gffn_ref, wq_ref, k1_ref, k2_ref,
                  x1_ref, t_ref, s1_ref, s2_ref):
    m = (ga_ref[...].astype(F32) * pa_ref[...].astype(F32)
         + gb_ref[...].astype(F32) * pb_ref[...].astype(F32))
    x1 = x_ref[...] + jnp.dot(m.astype(BF16), wo_ref[...], preferred_element_type=F32)
    x1_ref[...] = x1
    t = _rms(x1) * gffn_ref[...]
    t_ref[...] = t
    qq = jnp.dot(t.astype(BF16), wq_ref[...], preferred_element_type=F32).astype(BF16)
    nt = (((1,), (1,)), ((), ()))
    for h in range(P_HEADS):
        base = h * 2 * P_HALF
        s1_ref[h] = lax.dot_general(k1_ref[...], qq[:, base:base + P_HALF], nt, preferred_element_type=F32)
        s2_ref[h] = lax.dot_general(k2_ref[...], qq[:, base + P_HALF:base + 2 * P_HALF], nt,
                                    preferred_element_type=F32)


def _merge(z_act, pa, pb, x, w_out_bf, g_ffn, w_query_bf, k1_bf, k2_bf, tm):
    n = x.shape[0]
    tok = lambda i: (i, 0)
    const = lambda i: (0, 0)
    return pl.pallas_call(
        _merge_kernel,
        grid=(n // tm,),
        in_specs=[
            pl.BlockSpec((tm, D_MODEL), lambda i: (i, 8)),
            pl.BlockSpec((tm, D_MODEL), lambda i: (i, 9)),
            pl.BlockSpec((tm, D_MODEL), tok),
            pl.BlockSpec((tm, D_MODEL), tok),
            pl.BlockSpec((tm, D_MODEL), tok),
            pl.BlockSpec((D_MODEL, D_MODEL), const),
            pl.BlockSpec((1, D_MODEL), const),
            pl.BlockSpec((D_MODEL, 2 * P_HALF * P_HEADS), const),
            pl.BlockSpec((P_NKEYS, P_HALF), const),
            pl.BlockSpec((P_NKEYS, P_HALF), const),
        ],
        out_specs=[
            pl.BlockSpec((tm, D_MODEL), tok),
            pl.BlockSpec((tm, D_MODEL), tok),
            pl.BlockSpec((P_HEADS, P_NKEYS, tm), lambda i: (0, 0, i)),
            pl.BlockSpec((P_HEADS, P_NKEYS, tm), lambda i: (0, 0, i)),
        ],
        out_shape=[
            jax.ShapeDtypeStruct((n, D_MODEL), F32),
            jax.ShapeDtypeStruct((n, D_MODEL), F32),
            jax.ShapeDtypeStruct((P_HEADS, P_NKEYS, n), F32),
            jax.ShapeDtypeStruct((P_HEADS, P_NKEYS, n), F32),
        ],
        compiler_params=_cparams("parallel"),
        name="merge",
    )(z_act, z_act, pa, pb, x, w_out_bf, g_ffn.reshape(1, -1), w_query_bf, k1_bf, k2_bf)


def _extract_top(vals, ids, count, big_id):
    out_v, out_i = [], []
    for _ in range(count):
        m = jnp.max(vals, axis=0, keepdims=True)
        sel = jnp.min(jnp.where(vals == m, ids, big_id), axis=0, keepdims=True)
        out_v.append(m)
        out_i.append(sel)
        vals = jnp.where(ids == sel, -jnp.inf, vals)
    return out_v, out_i


def _topk_kernel(s1_ref, s2_ref, idx_ref, w_ref, idx_scr):
    tl = s1_ref.shape[2]
    key_id = lax.broadcasted_iota(jnp.int32, (P_NKEYS, tl), 0).astype(F32)
    sub8 = lax.broadcasted_iota(jnp.int32, (8, tl), 0).astype(F32)
    sub16 = lax.broadcasted_iota(jnp.int32, (P_TOPK, tl), 0).astype(F32)

    def head_body(h, carry):
        v1, i1 = _extract_top(s1_ref[h], key_id, P_TOPK, P_NKEYS)
        v2, i2 = _extract_top(s2_ref[h], key_id, P_TOPK, P_NKEYS)
        v2_all = jnp.concatenate(v2, axis=0)
        i2_all = jnp.concatenate(i2, axis=0)
        cand_v, cand_flat, cand_e = [], [], []
        for a in range(P_TOPK):
            nb = P_TOPK // (a + 1)
            if nb > 8:
                sub, v2a, i2a = sub16, v2_all, i2_all
            else:
                sub, v2a, i2a = sub8, v2_all[:8], i2_all[:8]
            ok = sub < nb
            cand_v.append(jnp.where(ok, v1[a] + v2a, -jnp.inf))
            cand_flat.append(a * P_TOPK + sub)
            cand_e.append(i1[a] * P_NKEYS + i2a)
        cv = jnp.concatenate(cand_v, axis=0)
        cf = jnp.concatenate(cand_flat, axis=0)
        ce = jnp.concatenate(cand_e, axis=0)
        big = P_TOPK * P_TOPK
        vals, experts = [], []
        for _ in range(P_TOPK):
            m = jnp.max(cv, axis=0, keepdims=True)
            sel = jnp.min(jnp.where(cv == m, cf, big), axis=0, keepdims=True)
            hit = cf == sel
            vals.append(m)
            experts.append(jnp.max(jnp.where(hit, ce, -1.0), axis=0, keepdims=True))
            cv = jnp.where(hit, -jnp.inf, cv)
        vals = jnp.concatenate(vals, axis=0)
        e = jnp.exp(vals - vals[0:1])
        w = e / jnp.sum(e, axis=0, keepdims=True)
        rows = pl.ds(pl.multiple_of(h * P_TOPK, P_TOPK), P_TOPK)
        idx_scr[rows, :] = jnp.concatenate(experts, axis=0).astype(jnp.int32) * TABLE_ROWS_PER_EXPERT
        w_ref[h] = w
        return carry

    lax.fori_loop(0, P_HEADS, head_body, 0)
    idx_ref[...] = idx_scr[...].T


def _topk(s1t, s2t, tl):
    n = s1t.shape[2]
    sc = pl.BlockSpec((P_HEADS, P_NKEYS, tl), lambda i: (0, 0, i))
    idx, w = pl.pallas_call(
        _topk_kernel,
        grid=(n // tl,),
        in_specs=[sc, sc],
        out_specs=[
            pl.BlockSpec((tl, P_PAIRS), lambda i: (i, 0)),
            pl.BlockSpec((P_HEADS, P_TOPK, tl), lambda i: (0, 0, i)),
        ],
        out_shape=[
            jax.ShapeDtypeStruct((n, P_PAIRS), jnp.int32),
            jax.ShapeDtypeStruct((P_HEADS, P_TOPK, n), F32),
        ],
        scratch_shapes=[pltpu.VMEM((P_PAIRS, tl), jnp.int32)],
        compiler_params=_cparams("parallel"),
        name="topk",
    )(s1t, s2t)
    return idx, w.reshape(P_PAIRS, n)


def _unpack_rows(x):
    lo = lax.bitcast_convert_type(x << 16, F32)
    hi = lax.bitcast_convert_type(x & jnp.uint32(0xFFFF0000), F32)
    return lo, hi


def _copy_rows(tab_ref, idx_ref, t, dst_ref):
    rpe = TABLE_ROWS_PER_EXPERT
    for k in range(P_PAIRS):
        row = pl.multiple_of(idx_ref[t, k], rpe)
        dst_ref[k * rpe:(k + 1) * rpe, :] = tab_ref[pl.ds(row, rpe), :]


def _peer_u_kernel(idx_ref, t_ref, w_ref, tab_ref, c_ref, *g_scrs):
    tb = t_ref.shape[0]
    rpe = TABLE_ROWS_PER_EXPERT
    lane = lax.broadcasted_iota(jnp.int32, (P_PAIRS, tb), 1)

    def group_body(g, acc):
        for u, g_scr in enumerate(g_scrs):
            t = g * len(g_scrs) + u
            _copy_rows(tab_ref, idx_ref, t, g_scr)
            part = None
            for r in range(rpe):
                lo, hi = _unpack_rows(g_scr[pl.ds(r, P_PAIRS, stride=rpe), :])
                term = lo * t_ref[t, pl.ds(r, 1), :] + hi * t_ref[t, pl.ds(r + rpe, 1), :]
                part = term if part is None else part + term
            col = jnp.sum(part, axis=1, keepdims=True)
            acc = jnp.where(lane == t, col, acc)
        return acc

    a = lax.fori_loop(0, tb // len(g_scrs), group_body, jnp.zeros((P_PAIRS, tb), F32))
    c_ref[...] = (w_ref[...] * jax.nn.gelu(a)).T


def _peer_u(idx_t, t3, w_t, table, tb):
    n = t3.shape[0]
    return pl.pallas_call(
        _peer_u_kernel,
        grid=(n // tb,),
        in_specs=[
            pl.BlockSpec((tb, P_PAIRS), lambda i: (i, 0), memory_space=pltpu.SMEM),
            pl.BlockSpec((tb, 8, LANES), lambda i: (i, 0, 0)),
            pl.BlockSpec((P_PAIRS, tb), lambda i: (0, i)),
            pl.BlockSpec(table.shape, lambda i: (0, 0), pipeline_mode=pl.Buffered(1)),
        ],
        out_specs=pl.BlockSpec((tb, P_PAIRS), lambda i: (i, 0)),
        out_shape=jax.ShapeDtypeStruct((n, P_PAIRS), F32),
        scratch_shapes=[pltpu.VMEM((P_PAIRS * TABLE_ROWS_PER_EXPERT, LANES), jnp.uint32)] * PEER_TOKENS_PER_ITER,
        compiler_params=_cparams("arbitrary"),
        name="peer_u",
    )(idx_t, t3, w_t, table)


def _peer_v_kernel(idx_ref, c_ref, x1_ref, gfin_ref, tab_ref, y_ref, *g_scrs):
    tb = x1_ref.shape[0]
    rpe = TABLE_ROWS_PER_EXPERT

    def group_body(g, carry):
        for u, g_scr in enumerate(g_scrs):
            t = g * len(g_scrs) + u
            _copy_rows(tab_ref, idx_ref, t, g_scr)
            cmat = jnp.broadcast_to(c_ref[pl.ds(t, 1), :], (P_PAIRS, P_PAIRS)).T
            lo_rows, hi_rows = [], []
            for r in range(rpe):
                lo, hi = _unpack_rows(g_scr[pl.ds(r, P_PAIRS, stride=rpe), :])
                lo_rows.append(jnp.sum(lo * cmat, axis=0, keepdims=True))
                hi_rows.append(jnp.sum(hi * cmat, axis=0, keepdims=True))
            peer = jnp.concatenate(lo_rows + hi_rows, axis=0)
            y_ref[t] = x1_ref[t] + peer
        return carry

    lax.fori_loop(0, tb // len(g_scrs), group_body, 0)
    x2 = y_ref[...]
    ms = jnp.sum(jnp.sum(x2 * x2, axis=2, keepdims=True), axis=1, keepdims=True) / D_MODEL
    y_ref[...] = x2 * lax.rsqrt(ms + EPS) * gfin_ref[...]


def _peer_v(idx_t, c_t, x1_3, g_final, table, tb):
    n = x1_3.shape[0]
    smem = pl.BlockSpec((tb, P_PAIRS), lambda i: (i, 0), memory_space=pltpu.SMEM)
    tok = pl.BlockSpec((tb, 8, LANES), lambda i: (i, 0, 0))
    return pl.pallas_call(
        _peer_v_kernel,
        grid=(n // tb,),
        in_specs=[
            smem, pl.BlockSpec((tb, P_PAIRS), lambda i: (i, 0)), tok,
            pl.BlockSpec((8, LANES), lambda i: (0, 0)),
            pl.BlockSpec(table.shape, lambda i: (0, 0), pipeline_mode=pl.Buffered(1)),
        ],
        out_specs=tok,
        out_shape=jax.ShapeDtypeStruct((n, 8, LANES), F32),
        scratch_shapes=[pltpu.VMEM((P_PAIRS * TABLE_ROWS_PER_EXPERT, LANES), jnp.uint32)] * PEER_TOKENS_PER_ITER,
        compiler_params=_cparams("arbitrary"),
        name="peer_v",
    )(idx_t, c_t, x1_3, g_final.reshape(8, LANES), table)


def _pack_table(e):
    bits = lax.bitcast_convert_type(e.astype(BF16), jnp.uint16).astype(jnp.uint32)
    half = D_MODEL // 2
    words = bits[:, :half] | (bits[:, half:] << 16)
    return words.reshape(e.shape[0] * TABLE_ROWS_PER_EXPERT, LANES)


def _rope_tables(pos):
    half = B_DK // 2
    inv = 1.0 / (ROPE_BASE ** jnp.linspace(0.0, 1.0, half, dtype=F32))
    ang = pos.astype(F32)[:, None] * inv[None, :]
    return jnp.cos(ang), jnp.sin(ang)


def _layer(x, n_seq, seq_len, pos0, s0, p, tabs, g_final):
    n = x.shape[0]
    tm_in = min(n, 1024)
    rows = 256
    ac = min(seq_len, A_CHUNK)
    chunk = min(seq_len, 256)
    assert seq_len % chunk == 0 and rows % ac == 0 and n % rows == 0 and n % tm_in == 0
    assert seq_len <= RET_CHUNK or seq_len % RET_CHUNK == 0

    pos = pos0 + jnp.arange(seq_len)
    if seq_len < tm_in:
        pos = jnp.tile(pos, tm_in // seq_len)
    cos_t, sin_t = _rope_tables(pos)
    z_act, v_norm = _inproj(x, p["g_mix"], p["w_in"], p["g_sgu"], p["b_gate"], cos_t, sin_t, tm_in)

    w_tiled = jnp.tile(p["w_s"][:, :ac, :ac], (1, rows // ac, rows // ac))
    bias_full = jnp.repeat(jnp.tile(p["b_s"][:, :ac].T, (rows // ac, 1)), A_GDIM, axis=1)
    pa = _sgu(z_act, w_tiled, bias_full, p["w_proj_a"], ac, rows)
    pb, s_new = _ret(z_act, s0, p["w_proj_b"], n_seq, seq_len, chunk)

    x1, t, s1t, s2t = _merge(z_act, pa, pb, x, p["w_out"], p["g_ffn"], p["w_query"], p["k1"], p["k2"], rows)
    idx_t, w_t = _topk(s1t, s2t, LANES)
    c_t = _peer_u(idx_t, t.reshape(n, 8, LANES), w_t, tabs[0], LANES)
    y = _peer_v(idx_t, c_t, x1.reshape(n, 8, LANES), g_final, tabs[1], LANES)
    return y.reshape(n, D_MODEL), s_new, v_norm


def kernel(x_prompt, x_sample, state_ret, w_in, w_s, b_s, g_sgu, w_proj_a, w_proj_b, b_gate, w_out, g_mix,
           g_ffn, w_query, sub_keys_1, sub_keys_2, expert_u, expert_v, g_final):
    assert w_in.shape[0] == 1, "single-layer trunk only"
    bp, lp, _ = x_prompt.shape
    bs, ls, _ = x_sample.shape
    past_len = 1024
    p = dict(
        g_mix=g_mix[0], w_in=w_in[0].astype(BF16), g_sgu=g_sgu[0], b_gate=b_gate[0],
        w_s=w_s[0], b_s=b_s[0], w_proj_a=w_proj_a[0].astype(BF16), w_proj_b=w_proj_b[0].astype(BF16),
        w_out=w_out[0].astype(BF16), g_ffn=g_ffn[0], w_query=w_query[0].astype(BF16),
        k1=sub_keys_1[0].astype(BF16), k2=sub_keys_2[0].astype(BF16))
    tabs = (_pack_table(expert_u[0]), _pack_table(expert_v[0]))
    s0p = jnp.zeros((bp, B_HEADS, B_DK, B_DV), F32)
    yp, sp, _ = _layer(x_prompt.reshape(bp * lp, D_MODEL), bp, lp, 0, s0p, p, tabs, g_final)
    ys, ss, vs = _layer(x_sample.reshape(bs * ls, D_MODEL), bs, ls, past_len, state_ret[0], p, tabs, g_final)
    return (yp.reshape(bp, lp, D_MODEL), ys.reshape(bs, ls, D_MODEL), sp[None], ss[None],
            vs.reshape(1, bs, ls, D_MODEL))
```

```python
import functools

import jax
import jax.numpy as jnp
from jax import lax
from jax.experimental import pallas as pl
from jax.experimental.pallas import tpu as pltpu

F32 = jnp.float32
BF16 = jnp.bfloat16

D_MODEL = 1024
EPS = 1e-6
A_CHUNK = 128
A_GROUPS = 8
A_GDIM = D_MODEL // A_GROUPS
B_HEADS = 4
B_DK = 256
B_DV = 512
B_QK = B_HEADS * B_DK
B_V = B_HEADS * B_DV
ROPE_BASE = 10000.0
RET_CHUNK = 64
IN_COLS = 10 * D_MODEL
P_HEADS = 8
P_NKEYS = 128
P_HALF = 128
P_TOPK = 16
P_PAIRS = P_HEADS * P_TOPK

LANES = 128
TABLE_ROWS_PER_EXPERT = D_MODEL // (2 * LANES)
PEER_TOKENS_PER_ITER = 8
VMEM_LIMIT = 56 * 1024 * 1024


def _cparams(*sem):
    return pltpu.CompilerParams(dimension_semantics=sem, vmem_limit_bytes=VMEM_LIMIT)


def _rms(x):
    return x * lax.rsqrt(jnp.mean(x * x, axis=-1, keepdims=True) + EPS)


def _inproj_kernel(x_ref, gmix_ref, w_ref, gsgu_ref, bgate_ref, cos_ref, sin_ref, z_ref, vn_ref, h_scr):
    j = pl.program_id(1)

    @pl.when(j == 0)
    def _():
        h = _rms(x_ref[...]) * gmix_ref[...]
        h_scr[...] = h.astype(BF16)

    z = jnp.dot(h_scr[...], w_ref[...], preferred_element_type=F32)

    @pl.when(j == 0)
    def _():
        z_ref[...] = jax.nn.gelu(z).astype(BF16)

    @pl.when(j == 1)
    def _():
        vn = _rms(jax.nn.gelu(z)) * gsgu_ref[...]
        vn_ref[...] = vn
        z_ref[...] = vn.astype(BF16)

    def rotary(scale):
        cos = cos_ref[...]
        sin = sin_ref[...]
        half = B_DK // 2
        for h in range(B_HEADS):
            x1 = z[:, h * B_DK:h * B_DK + half]
            x2 = z[:, h * B_DK + half:(h + 1) * B_DK]
            z_ref[:, h * B_DK:h * B_DK + half] = ((x1 * cos - x2 * sin) * scale).astype(BF16)
            z_ref[:, h * B_DK + half:(h + 1) * B_DK] = ((x1 * sin + x2 * cos) * scale).astype(BF16)

    @pl.when(j == 2)
    def _():
        rotary(1.0)

    @pl.when(j == 3)
    def _():
        rotary(B_DK ** -0.5)

    @pl.when((j == 4) | (j == 5))
    def _():
        z_ref[...] = z.astype(BF16)

    @pl.when((j == 6) | (j == 7))
    def _():
        z_ref[...] = jax.nn.silu(z).astype(BF16)

    @pl.when(j == 8)
    def _():
        z_ref[...] = jax.nn.sigmoid(z + bgate_ref[:, :D_MODEL]).astype(BF16)

    @pl.when(j == 9)
    def _():
        z_ref[...] = jax.nn.sigmoid(z + bgate_ref[:, D_MODEL:]).astype(BF16)


def _inproj(x, g_mix, w_in_bf, g_sgu, b_gate, cos_t, sin_t, tm):
    n = x.shape[0]
    n_pos_blocks = cos_t.shape[0] // tm
    const = lambda i, j: (0, 0)
    return pl.pallas_call(
        _inproj_kernel,
        grid=(n // tm, IN_COLS // D_MODEL),
        in_specs=[
            pl.BlockSpec((tm, D_MODEL), lambda i, j: (i, 0)),
            pl.BlockSpec((1, D_MODEL), const),
            pl.BlockSpec((D_MODEL, D_MODEL), lambda i, j: (0, j)),
            pl.BlockSpec((1, D_MODEL), const),
            pl.BlockSpec((1, 2 * D_MODEL), const),
            pl.BlockSpec((tm, B_DK // 2), lambda i, j: (i % n_pos_blocks, 0)),
            pl.BlockSpec((tm, B_DK // 2), lambda i, j: (i % n_pos_blocks, 0)),
        ],
        out_specs=[
            pl.BlockSpec((tm, D_MODEL), lambda i, j: (i, j)),
            pl.BlockSpec((tm, D_MODEL), lambda i, j: (i, 0)),
        ],
        out_shape=[
            jax.ShapeDtypeStruct((n, IN_COLS), BF16),
            jax.ShapeDtypeStruct((n, D_MODEL), F32),
        ],
        scratch_shapes=[pltpu.VMEM((tm, D_MODEL), BF16)],
        compiler_params=_cparams("parallel", "arbitrary"),
        name="inproj",
    )(x, g_mix.reshape(1, -1), w_in_bf, g_sgu.reshape(1, -1), b_gate.reshape(1, -1), cos_t, sin_t)


def _sgu_kernel(ac, u_ref, v_ref, wt_ref, bias_ref, wp_ref, pa_ref, wm_scr):
    r = u_ref.shape[0]

    @pl.when(pl.program_id(0) == 0)
    def _():
        row = lax.broadcasted_iota(jnp.int32, (r, r), 0)
        col = lax.broadcasted_iota(jnp.int32, (r, r), 1)
        keep = (row // ac == col // ac) & (col <= row)
        for g in range(A_GROUPS):
            wm_scr[g] = jnp.where(keep, wt_ref[g], 0.0).astype(BF16)

    v = v_ref[...]
    mixed = jnp.concatenate(
        [jnp.dot(wm_scr[g], v[:, g * A_GDIM:(g + 1) * A_GDIM], preferred_element_type=F32)
         for g in range(A_GROUPS)], axis=1) + bias_ref[...]
    y = (u_ref[...].astype(F32) * mixed).astype(BF16)
    pa_ref[...] = jnp.dot(y, wp_ref[...], preferred_element_type=F32).astype(BF16)


def _sgu(z_act, w_tiled, bias_full, w_proj_a_bf, ac, r):
    n = z_act.shape[0]
    return pl.pallas_call(
        functools.partial(_sgu_kernel, ac),
        grid=(n // r,),
        in_specs=[
            pl.BlockSpec((r, D_MODEL), lambda i: (i, 0)),
            pl.BlockSpec((r, D_MODEL), lambda i: (i, 1)),
            pl.BlockSpec((A_GROUPS, r, r), lambda i: (0, 0, 0)),
            pl.BlockSpec((r, D_MODEL), lambda i: (0, 0)),
            pl.BlockSpec((D_MODEL, D_MODEL), lambda i: (0, 0)),
        ],
        out_specs=pl.BlockSpec((r, D_MODEL), lambda i: (i, 0)),
        out_shape=jax.ShapeDtypeStruct((n, D_MODEL), BF16),
        scratch_shapes=[pltpu.VMEM((A_GROUPS, r, r), BF16)],
        compiler_params=_cparams("arbitrary"),
        name="sgu",
    )(z_act, z_act, w_tiled, bias_full, w_proj_a_bf)


def _ret_kernel(q_ref, k_ref, v_ref, g_ref, s0_ref, dmask_ref, qdec_ref, kdec_ref, gc_ref,
                wp_ref, pb_ref, sout_ref, s_scr):
    c = pl.program_id(1)

    @pl.when(c == 0)
    def _():
        s_scr[...] = s0_ref[0]

    acc = None
    for h in range(B_HEADS):
        q = q_ref[:, h * B_DK:(h + 1) * B_DK]
        k = k_ref[:, h * B_DK:(h + 1) * B_DK]
        v = v_ref[:, h * B_DV:(h + 1) * B_DV]
        s = s_scr[h]
        scores = lax.dot_general(q, k, (((1,), (1,)), ((), ())), preferred_element_type=F32) * dmask_ref[h]
        o = jnp.dot(scores.astype(BF16), v, preferred_element_type=F32)
        qd = (q.astype(F32) * qdec_ref[h]).astype(BF16)
        o = o + jnp.dot(qd, s.astype(BF16), preferred_element_type=F32)
        kd = (k.astype(F32) * kdec_ref[h]).astype(BF16)
        s_scr[h] = gc_ref[h] * s + lax.dot_general(kd, v, (((0,), (0,)), ((), ())), preferred_element_type=F32)
        y = (g_ref[:, h * B_DV:(h + 1) * B_DV].astype(F32) * _rms(o)).astype(BF16)
        part = jnp.dot(y, wp_ref[h * B_DV:(h + 1) * B_DV, :], preferred_element_type=F32)
        acc = part if acc is None else acc + part
    pb_ref[...] = acc.astype(BF16)

    @pl.when(c == pl.num_programs(1) - 1)
    def _():
        sout_ref[0] = s_scr[...]


def _ret(z_act, s0, w_proj_b_bf, n_seq, seq_len, chunk):
    n = z_act.shape[0]
    nc = seq_len // chunk
    dmask, qdec, kdec, gc = _ret_consts(chunk)
    row = lambda b, c: b * nc + c
    state_spec = pl.BlockSpec((1, B_HEADS, B_DK, B_DV), lambda b, c: (b, 0, 0, 0))
    full = lambda shape: pl.BlockSpec(shape, lambda b, c: (0,) * len(shape))
    return pl.pallas_call(
        _ret_kernel,
        grid=(n_seq, nc),
        in_specs=[
            pl.BlockSpec((chunk, B_QK), lambda b, c: (row(b, c), 2)),
            pl.BlockSpec((chunk, B_QK), lambda b, c: (row(b, c), 3)),
            pl.BlockSpec((chunk, B_V), lambda b, c: (row(b, c), 2)),
            pl.BlockSpec((chunk, B_V), lambda b, c: (row(b, c), 3)),
            state_spec,
            full((B_HEADS, chunk, chunk)),
            full((B_HEADS, chunk, 1)),
            full((B_HEADS, chunk, 1)),
            full((B_HEADS, 1, 1)),
            full((B_V, D_MODEL)),
        ],
        out_specs=[
            pl.BlockSpec((chunk, D_MODEL), lambda b, c: (row(b, c), 0)),
            state_spec,
        ],
        out_shape=[
            jax.ShapeDtypeStruct((n, D_MODEL), BF16),
            jax.ShapeDtypeStruct((n_seq, B_HEADS, B_DK, B_DV), F32),
        ],
        scratch_shapes=[pltpu.VMEM((B_HEADS, B_DK, B_DV), F32)],
        compiler_params=_cparams("parallel", "arbitrary"),
        name="ret",
    )(z_act, z_act, z_act, z_act, s0, dmask, qdec, kdec, gc, w_proj_b_bf)


def _ret_consts(chunk):
    log_g = jnp.log(1.0 - 2.0 ** (-5.0 - jnp.arange(B_HEADS, dtype=F32)))
    idx = jnp.arange(chunk, dtype=F32)
    diff = idx[:, None] - idx[None, :]
    lg = log_g[:, None]
    dmask = jnp.where(diff[None] >= 0.0, jnp.exp(lg[:, :, None] * jnp.maximum(diff, 0.0)[None]), 0.0)
    qdec = jnp.exp(lg * (idx + 1.0))[:, :, None]
    kdec = jnp.exp(lg * (chunk - 1.0 - idx))[:, :, None]
    gc = jnp.exp(log_g * chunk)[:, None, None]
    return dmask, qdec, kdec, gc


def _merge_kernel(ga_ref, gb_ref, pa_ref, pb_ref, x_ref, wo_ref, gffn_ref, wq_ref, k1_ref, k2_ref,
                  x1_ref, t_ref, s1_ref, s2_ref):
    m = (ga_ref[...].astype(F32) * pa_ref[...].astype(F32)
         + gb_ref[...].astype(F32) * pb_ref[...].astype(F32))
    x1 = x_ref[...] + jnp.dot(m.astype(BF16), wo_ref[...], preferred_element_type=F32)
    x1_ref[...] = x1
    t = _rms(x1) * gffn_ref[...]
    t_ref[...] = t
    qq = jnp.dot(t.astype(BF16), wq_ref[...], preferred_element_type=F32).astype(BF16)
    nt = (((1,), (1,)), ((), ()))
    for h in range(P_HEADS):
        base = h * 2 * P_HALF
        s1_ref[h] = lax.dot_general(k1_ref[...], qq[:, base:base + P_HALF], nt, preferred_element_type=F32)
        s2_ref[h] = lax.dot_general(k2_ref[...], qq[:, base + P_HALF:base + 2 * P_HALF], nt,
                                    preferred_element_type=F32)


def _merge(z_act, pa, pb, x, w_out_bf, g_ffn, w_query_bf, k1_bf, k2_bf, tm):
    n = x.shape[0]
    tok = lambda i: (i, 0)
    const = lambda i: (0, 0)
    return pl.pallas_call(
        _merge_kernel,
        grid=(n // tm,),
        in_specs=[
            pl.BlockSpec((tm, D_MODEL), lambda i: (i, 8)),
            pl.BlockSpec((tm, D_MODEL), lambda i: (i, 9)),
            pl.BlockSpec((tm, D_MODEL), tok),
            pl.BlockSpec((tm, D_MODEL), tok),
            pl.BlockSpec((tm, D_MODEL), tok),
            pl.BlockSpec((D_MODEL, D_MODEL), const),
            pl.BlockSpec((1, D_MODEL), const),
            pl.BlockSpec((D_MODEL, 2 * P_HALF * P_HEADS), const),
            pl.BlockSpec((P_NKEYS, P_HALF), const),
            pl.BlockSpec((P_NKEYS, P_HALF), const),
        ],
        out_specs=[
            pl.BlockSpec((tm, D_MODEL), tok),
            pl.BlockSpec((tm, D_MODEL), tok),
            pl.BlockSpec((P_HEADS, P_NKEYS, tm), lambda i: (0, 0, i)),
            pl.BlockSpec((P_HEADS, P_NKEYS, tm), lambda i: (0, 0, i)),
        ],
        out_shape=[
            jax.ShapeDtypeStruct((n, D_MODEL), F32),
            jax.ShapeDtypeStruct((n, D_MODEL), F32),
            jax.ShapeDtypeStruct((P_HEADS, P_NKEYS, n), F32),
            jax.ShapeDtypeStruct((P_HEADS, P_NKEYS, n), F32),
        ],
        compiler_params=_cparams("parallel"),
        name="merge",
    )(z_act, z_act, pa, pb, x, w_out_bf, g_ffn.reshape(1, -1), w_query_bf, k1_bf, k2_bf)


def _extract_top(vals, ids, count, big_id):
    out_v, out_i = [], []
    for _ in range(count):
        m = jnp.max(vals, axis=0, keepdims=True)
        sel = jnp.min(jnp.where(vals == m, ids, big_id), axis=0, keepdims=True)
        out_v.append(m)
        out_i.append(sel)
        vals = jnp.where(ids == sel, -jnp.inf, vals)
    return out_v, out_i


def _topk_kernel(s1_ref, s2_ref, idx_ref, w_ref, idx_scr, w_scr):
    tl = s1_ref.shape[2]
    key_id = lax.broadcasted_iota(jnp.int32, (P_NKEYS, tl), 0).astype(F32)
    sub8 = lax.broadcasted_iota(jnp.int32, (8, tl), 0).astype(F32)
    sub16 = lax.broadcasted_iota(jnp.int32, (P_TOPK, tl), 0).astype(F32)

    def head_body(h, carry):
        v1, i1 = _extract_top(s1_ref[h], key_id, P_TOPK, P_NKEYS)
        v2, i2 = _extract_top(s2_ref[h], key_id, P_TOPK, P_NKEYS)
        v2_all = jnp.concatenate(v2, axis=0)
        i2_all = jnp.concatenate(i2, axis=0)
        cand_v, cand_flat, cand_e = [], [], []
        for a in range(P_TOPK):
            nb = P_TOPK // (a + 1)
            if nb > 8:
                sub, v2a, i2a = sub16, v2_all, i2_all
            else:
                sub, v2a, i2a = sub8, v2_all[:8], i2_all[:8]
            ok = sub < nb
            cand_v.append(jnp.where(ok, v1[a] + v2a, -jnp.inf))
            cand_flat.append(a * P_TOPK + sub)
            cand_e.append(i1[a] * P_NKEYS + i2a)
        cv = jnp.concatenate(cand_v, axis=0)
        cf = jnp.concatenate(cand_flat, axis=0)
        ce = jnp.concatenate(cand_e, axis=0)
        big = P_TOPK * P_TOPK
        vals, experts = [], []
        for _ in range(P_TOPK):
            m = jnp.max(cv, axis=0, keepdims=True)
            sel = jnp.min(jnp.where(cv == m, cf, big), axis=0, keepdims=True)
            hit = cf == sel
            vals.append(m)
            experts.append(jnp.max(jnp.where(hit, ce, -1.0), axis=0, keepdims=True))
            cv = jnp.where(hit, -jnp.inf, cv)
        vals = jnp.concatenate(vals, axis=0)
        e = jnp.exp(vals - vals[0:1])
        w = e / jnp.sum(e, axis=0, keepdims=True)
        rows = pl.ds(pl.multiple_of(h * P_TOPK, P_TOPK), P_TOPK)
        idx_scr[rows, :] = jnp.concatenate(experts, axis=0).astype(jnp.int32) * TABLE_ROWS_PER_EXPERT
        w_scr[rows, :] = w
        return carry

    lax.fori_loop(0, P_HEADS, head_body, 0)
    idx_ref[...] = idx_scr[...].T
    w_ref[...] = w_scr[...].T


def _topk(s1t, s2t, tl):
    n = s1t.shape[2]
    sc = pl.BlockSpec((P_HEADS, P_NKEYS, tl), lambda i: (0, 0, i))
    tok = pl.BlockSpec((tl, P_PAIRS), lambda i: (i, 0))
    return pl.pallas_call(
        _topk_kernel,
        grid=(n // tl,),
        in_specs=[sc, sc],
        out_specs=[tok, tok],
        out_shape=[
            jax.ShapeDtypeStruct((n, P_PAIRS), jnp.int32),
            jax.ShapeDtypeStruct((n, P_PAIRS), F32),
        ],
        scratch_shapes=[pltpu.VMEM((P_PAIRS, tl), jnp.int32), pltpu.VMEM((P_PAIRS, tl), F32)],
        compiler_params=_cparams("parallel"),
        name="topk",
    )(s1t, s2t)


def _unpack_rows(x):
    lo = lax.bitcast_convert_type(x << 16, F32)
    hi = lax.bitcast_convert_type(x & jnp.uint32(0xFFFF0000), F32)
    return lo, hi


def _copy_rows(tab_ref, idx_ref, t, dst_ref):
    rpe = TABLE_ROWS_PER_EXPERT
    for k in range(P_PAIRS):
        row = pl.multiple_of(idx_ref[t, k], rpe)
        dst_ref[k * rpe:(k + 1) * rpe, :] = tab_ref[pl.ds(row, rpe), :]


def _peer_u_kernel(idx_ref, t_ref, w_ref, tab_ref, c_ref, *g_scrs):
    tb = t_ref.shape[0]
    rpe = TABLE_ROWS_PER_EXPERT
    ones = jnp.ones((8, 2 * LANES), BF16)
    contract_lanes = (((1,), (1,)), ((), ()))

    def group_body(g, carry):
        for u, g_scr in enumerate(g_scrs):
            t = g * len(g_scrs) + u
            _copy_rows(tab_ref, idx_ref, t, g_scr)
            t_row = t_ref[pl.ds(t, 1), :]
            part = None
            for r in range(rpe):
                lo, hi = _unpack_rows(g_scr[pl.ds(r, P_PAIRS, stride=rpe), :])
                t_lo = t_row[:, r * LANES:(r + 1) * LANES]
                t_hi = t_row[:, (r + rpe) * LANES:(r + rpe + 1) * LANES]
                term = lo * t_lo + hi * t_hi
                part = term if part is None else part + term
            p_hi = part.astype(BF16)
            p_lo = (part - p_hi.astype(F32)).astype(BF16)
            sums = lax.dot_general(ones, jnp.concatenate([p_hi, p_lo], axis=1), contract_lanes,
                                   preferred_element_type=F32)
            c_ref[pl.ds(t, 1), :] = sums[0:1]
        return carry

    lax.fori_loop(0, tb // len(g_scrs), group_body, 0)
    c_ref[...] = w_ref[...] * jax.nn.gelu(c_ref[...])


def _peer_u(idx_t, t, w_t, table, tb):
    n = t.shape[0]
    return pl.pallas_call(
        _peer_u_kernel,
        grid=(n // tb,),
        in_specs=[
            pl.BlockSpec((tb, P_PAIRS), lambda i: (i, 0), memory_space=pltpu.SMEM),
            pl.BlockSpec((tb, D_MODEL), lambda i: (i, 0)),
            pl.BlockSpec((tb, P_PAIRS), lambda i: (i, 0)),
            pl.BlockSpec(table.shape, lambda i: (0, 0), pipeline_mode=pl.Buffered(1)),
        ],
        out_specs=pl.BlockSpec((tb, P_PAIRS), lambda i: (i, 0)),
        out_shape=jax.ShapeDtypeStruct((n, P_PAIRS), F32),
        scratch_shapes=[pltpu.VMEM((P_PAIRS * TABLE_ROWS_PER_EXPERT, LANES), jnp.uint32)] * PEER_TOKENS_PER_ITER,
        compiler_params=_cparams("arbitrary"),
        name="peer_u",
    )(idx_t, t, w_t, table)


def _peer_v_kernel(idx_ref, c_ref, x1_ref, gfin_ref, tab_ref, y_ref, *g_scrs):
    tb = x1_ref.shape[0]
    rpe = TABLE_ROWS_PER_EXPERT

    def group_body(g, carry):
        for u, g_scr in enumerate(g_scrs):
            t = g * len(g_scrs) + u
            _copy_rows(tab_ref, idx_ref, t, g_scr)
            cmat = jnp.broadcast_to(c_ref[pl.ds(t, 1), :], (P_PAIRS, P_PAIRS)).T
            lo_rows, hi_rows = [], []
            for r in range(rpe):
                lo, hi = _unpack_rows(g_scr[pl.ds(r, P_PAIRS, stride=rpe), :])
                lo_rows.append(jnp.sum(lo * cmat, axis=0, keepdims=True))
                hi_rows.append(jnp.sum(hi * cmat, axis=0, keepdims=True))
            y_ref[pl.ds(t, 1), :] = x1_ref[pl.ds(t, 1), :] + jnp.concatenate(lo_rows + hi_rows, axis=1)
        return carry

    lax.fori_loop(0, tb // len(g_scrs), group_body, 0)
    y_ref[...] = _rms(y_ref[...]) * gfin_ref[...]


def _peer_v(idx_t, c_t, x1, g_final, table, tb):
    n = x1.shape[0]
    smem = pl.BlockSpec((tb, P_PAIRS), lambda i: (i, 0), memory_space=pltpu.SMEM)
    tok = pl.BlockSpec((tb, D_MODEL), lambda i: (i, 0))
    return pl.pallas_call(
        _peer_v_kernel,
        grid=(n // tb,),
        in_specs=[
            smem, pl.BlockSpec((tb, P_PAIRS), lambda i: (i, 0)), tok,
            pl.BlockSpec((1, D_MODEL), lambda i: (0, 0)),
            pl.BlockSpec(table.shape, lambda i: (0, 0), pipeline_mode=pl.Buffered(1)),
        ],
        out_specs=tok,
        out_shape=jax.ShapeDtypeStruct((n, D_MODEL), F32),
        scratch_shapes=[pltpu.VMEM((P_PAIRS * TABLE_ROWS_PER_EXPERT, LANES), jnp.uint32)] * PEER_TOKENS_PER_ITER,
        compiler_params=_cparams("arbitrary"),
        name="peer_v",
    )(idx_t, c_t, x1, g_final.reshape(1, D_MODEL), table)


def _pack_table(e):
    bits = lax.bitcast_convert_type(e.astype(BF16), jnp.uint16).astype(jnp.uint32)
    half = D_MODEL // 2
    words = bits[:, :half] | (bits[:, half:] << 16)
    return words.reshape(e.shape[0] * TABLE_ROWS_PER_EXPERT, LANES)


def _rope_tables(pos):
    half = B_DK // 2
    inv = 1.0 / (ROPE_BASE ** jnp.linspace(0.0, 1.0, half, dtype=F32))
    ang = pos.astype(F32)[:, None] * inv[None, :]
    return jnp.cos(ang), jnp.sin(ang)


def _layer(x, n_seq, seq_len, pos0, s0, p, tabs, g_final):
    n = x.shape[0]
    tm_in = min(n, 1024)
    rows = 256
    ac = min(seq_len, A_CHUNK)
    chunk = min(seq_len, 256)
    assert seq_len % chunk == 0 and rows % ac == 0 and n % rows == 0 and n % tm_in == 0
    assert seq_len <= RET_CHUNK or seq_len % RET_CHUNK == 0

    pos = pos0 + jnp.arange(seq_len)
    if seq_len < tm_in:
        pos = jnp.tile(pos, tm_in // seq_len)
    cos_t, sin_t = _rope_tables(pos)
    z_act, v_norm = _inproj(x, p["g_mix"], p["w_in"], p["g_sgu"], p["b_gate"], cos_t, sin_t, tm_in)

    w_tiled = jnp.tile(p["w_s"][:, :ac, :ac], (1, rows // ac, rows // ac))
    bias_full = jnp.repeat(jnp.tile(p["b_s"][:, :ac].T, (rows // ac, 1)), A_GDIM, axis=1)
    pa = _sgu(z_act, w_tiled, bias_full, p["w_proj_a"], ac, rows)
    pb, s_new = _ret(z_act, s0, p["w_proj_b"], n_seq, seq_len, chunk)

    x1, t, s1t, s2t = _merge(z_act, pa, pb, x, p["w_out"], p["g_ffn"], p["w_query"], p["k1"], p["k2"], rows)
    idx_t, w_t = _topk(s1t, s2t, LANES)
    c_t = _peer_u(idx_t, t, w_t, tabs[0], LANES)
    y = _peer_v(idx_t, c_t, x1, g_final, tabs[1], LANES)
    return y, s_new, v_norm


def kernel(x_prompt, x_sample, state_ret, w_in, w_s, b_s, g_sgu, w_proj_a, w_proj_b, b_gate, w_out, g_mix,
           g_ffn, w_query, sub_keys_1, sub_keys_2, expert_u, expert_v, g_final):
    assert w_in.shape[0] == 1, "single-layer trunk only"
    bp, lp, _ = x_prompt.shape
    bs, ls, _ = x_sample.shape
    past_len = 1024
    p = dict(
        g_mix=g_mix[0], w_in=w_in[0].astype(BF16), g_sgu=g_sgu[0], b_gate=b_gate[0],
        w_s=w_s[0], b_s=b_s[0], w_proj_a=w_proj_a[0].astype(BF16), w_proj_b=w_proj_b[0].astype(BF16),
        w_out=w_out[0].astype(BF16), g_ffn=g_ffn[0], w_query=w_query[0].astype(BF16),
        k1=sub_keys_1[0].astype(BF16), k2=sub_keys_2[0].astype(BF16))
    tabs = (_pack_table(expert_u[0]), _pack_table(expert_v[0]))
    s0p = jnp.zeros((bp, B_HEADS, B_DK, B_DV), F32)
    yp, sp, _ = _layer(x_prompt.reshape(bp * lp, D_MODEL), bp, lp, 0, s0p, p, tabs, g_final)
    ys, ss, vs = _layer(x_sample.reshape(bs * ls, D_MODEL), bs, ls, past_len, state_ret[0], p, tabs, g_final)
    return (yp.reshape(bp, lp, D_MODEL), ys.reshape(bs, ls, D_MODEL), sp[None], ss[None],
            vs.reshape(1, bs, ls, D_MODEL))
```

```python
import functools

import jax
import jax.numpy as jnp
from jax import lax
from jax.experimental import pallas as pl
from jax.experimental.pallas import tpu as pltpu

F32 = jnp.float32
BF16 = jnp.bfloat16

D_MODEL = 1024
EPS = 1e-6
A_CHUNK = 128
A_GROUPS = 8
A_GDIM = D_MODEL // A_GROUPS
B_HEADS = 4
B_DK = 256
B_DV = 512
B_QK = B_HEADS * B_DK
B_V = B_HEADS * B_DV
ROPE_BASE = 10000.0
RET_CHUNK = 64
IN_COLS = 10 * D_MODEL
P_HEADS = 8
P_NKEYS = 128
P_HALF = 128
P_TOPK = 16
P_PAIRS = P_HEADS * P_TOPK

LANES = 128
TABLE_ROWS_PER_EXPERT = D_MODEL // (2 * LANES)
INPROJ_ROW_CHUNK = 256
MERGE_ROW_CHUNK = 128
PEER_TOKENS_PER_ITER = 8
VMEM_LIMIT = 56 * 1024 * 1024


def _cparams(*sem):
    return pltpu.CompilerParams(dimension_semantics=sem, vmem_limit_bytes=VMEM_LIMIT)


def _rms(x):
    return x * lax.rsqrt(jnp.mean(x * x, axis=-1, keepdims=True) + EPS)


def _inproj_kernel(x_ref, gmix_ref, w_ref, gsgu_ref, bgate_ref, cos_ref, sin_ref, z_ref, vn_ref, h_scr):
    j = pl.program_id(1)
    tm = x_ref.shape[0]

    @pl.when(j == 0)
    def _():
        h = _rms(x_ref[...]) * gmix_ref[...]
        h_scr[...] = h.astype(BF16)

    def per_chunk(epilogue):
        for c in range(tm // INPROJ_ROW_CHUNK):
            rows = slice(c * INPROJ_ROW_CHUNK, (c + 1) * INPROJ_ROW_CHUNK)
            epilogue(jnp.dot(h_scr[rows, :], w_ref[...], preferred_element_type=F32), rows)

    def store(f):
        def epilogue(z, rows):
            z_ref[rows, :] = f(z).astype(BF16)
        return epilogue

    def store_vnorm(z, rows):
        vn = _rms(jax.nn.gelu(z)) * gsgu_ref[...]
        vn_ref[rows, :] = vn
        z_ref[rows, :] = vn.astype(BF16)

    def store_rotary(scale):
        def epilogue(z, rows):
            cos = cos_ref[rows, :]
            sin = sin_ref[rows, :]
            half = B_DK // 2
            for h in range(B_HEADS):
                x1 = z[:, h * B_DK:h * B_DK + half]
                x2 = z[:, h * B_DK + half:(h + 1) * B_DK]
                z_ref[rows, h * B_DK:h * B_DK + half] = ((x1 * cos - x2 * sin) * scale).astype(BF16)
                z_ref[rows, h * B_DK + half:(h + 1) * B_DK] = ((x1 * sin + x2 * cos) * scale).astype(BF16)
        return epilogue

    pl.when(j == 0)(lambda: per_chunk(store(jax.nn.gelu)))
    pl.when(j == 1)(lambda: per_chunk(store_vnorm))
    pl.when(j == 2)(lambda: per_chunk(store_rotary(1.0)))
    pl.when(j == 3)(lambda: per_chunk(store_rotary(B_DK ** -0.5)))
    pl.when((j == 4) | (j == 5))(lambda: per_chunk(store(lambda z: z)))
    pl.when((j == 6) | (j == 7))(lambda: per_chunk(store(jax.nn.silu)))
    pl.when(j == 8)(lambda: per_chunk(store(lambda z: jax.nn.sigmoid(z + bgate_ref[:, :D_MODEL]))))
    pl.when(j == 9)(lambda: per_chunk(store(lambda z: jax.nn.sigmoid(z + bgate_ref[:, D_MODEL:]))))


def _inproj(x, g_mix, w_in_bf, g_sgu, b_gate, cos_t, sin_t, tm):
    n = x.shape[0]
    n_pos_blocks = cos_t.shape[0] // tm
    const = lambda i, j: (0, 0)
    return pl.pallas_call(
        _inproj_kernel,
        grid=(n // tm, IN_COLS // D_MODEL),
        in_specs=[
            pl.BlockSpec((tm, D_MODEL), lambda i, j: (i, 0)),
            pl.BlockSpec((1, D_MODEL), const),
            pl.BlockSpec((D_MODEL, D_MODEL), lambda i, j: (0, j)),
            pl.BlockSpec((1, D_MODEL), const),
            pl.BlockSpec((1, 2 * D_MODEL), const),
            pl.BlockSpec((tm, B_DK // 2), lambda i, j: (i % n_pos_blocks, 0)),
            pl.BlockSpec((tm, B_DK // 2), lambda i, j: (i % n_pos_blocks, 0)),
        ],
        out_specs=[
            pl.BlockSpec((tm, D_MODEL), lambda i, j: (i, j)),
            pl.BlockSpec((tm, D_MODEL), lambda i, j: (i, 0)),
        ],
        out_shape=[
            jax.ShapeDtypeStruct((n, IN_COLS), BF16),
            jax.ShapeDtypeStruct((n, D_MODEL), F32),
        ],
        scratch_shapes=[pltpu.VMEM((tm, D_MODEL), BF16)],
        compiler_params=_cparams("parallel", "arbitrary"),
        name="inproj",
    )(x, g_mix.reshape(1, -1), w_in_bf, g_sgu.reshape(1, -1), b_gate.reshape(1, -1), cos_t, sin_t)


def _sgu_kernel(ac, u_ref, v_ref, wt_ref, bias_ref, wp_ref, pa_ref, wm_scr):
    r = u_ref.shape[0]

    @pl.when(pl.program_id(0) == 0)
    def _():
        row = lax.broadcasted_iota(jnp.int32, (r, r), 0)
        col = lax.broadcasted_iota(jnp.int32, (r, r), 1)
        keep = (row // ac == col // ac) & (col <= row)
        for g in range(A_GROUPS):
            wm_scr[g] = jnp.where(keep, wt_ref[g], 0.0).astype(BF16)

    v = v_ref[...]
    mixed = jnp.concatenate(
        [jnp.dot(wm_scr[g], v[:, g * A_GDIM:(g + 1) * A_GDIM], preferred_element_type=F32)
         for g in range(A_GROUPS)], axis=1) + bias_ref[...]
    y = (u_ref[...].astype(F32) * mixed).astype(BF16)
    pa_ref[...] = jnp.dot(y, wp_ref[...], preferred_element_type=F32).astype(BF16)


def _sgu(z_act, w_tiled, bias_full, w_proj_a_bf, ac, r):
    n = z_act.shape[0]
    return pl.pallas_call(
        functools.partial(_sgu_kernel, ac),
        grid=(n // r,),
        in_specs=[
            pl.BlockSpec((r, D_MODEL), lambda i: (i, 0)),
            pl.BlockSpec((r, D_MODEL), lambda i: (i, 1)),
            pl.BlockSpec((A_GROUPS, r, r), lambda i: (0, 0, 0)),
            pl.BlockSpec((r, D_MODEL), lambda i: (0, 0)),
            pl.BlockSpec((D_MODEL, D_MODEL), lambda i: (0, 0)),
        ],
        out_specs=pl.BlockSpec((r, D_MODEL), lambda i: (i, 0)),
        out_shape=jax.ShapeDtypeStruct((n, D_MODEL), BF16),
        scratch_shapes=[pltpu.VMEM((A_GROUPS, r, r), BF16)],
        compiler_params=_cparams("arbitrary"),
        name="sgu",
    )(z_act, z_act, w_tiled, bias_full, w_proj_a_bf)


def _ret_kernel(q_ref, k_ref, v_ref, g_ref, s0_ref, dmask_ref, qdec_ref, kdec_ref, gc_ref,
                wp_ref, pb_ref, sout_ref, s_scr):
    c = pl.program_id(1)

    @pl.when(c == 0)
    def _():
        s_scr[...] = s0_ref[0]

    acc = None
    for h in range(B_HEADS):
        q = q_ref[:, h * B_DK:(h + 1) * B_DK]
        k = k_ref[:, h * B_DK:(h + 1) * B_DK]
        v = v_ref[:, h * B_DV:(h + 1) * B_DV]
        s = s_scr[h]
        scores = lax.dot_general(q, k, (((1,), (1,)), ((), ())), preferred_element_type=F32) * dmask_ref[h]
        o = jnp.dot(scores.astype(BF16), v, preferred_element_type=F32)
        qd = (q.astype(F32) * qdec_ref[h]).astype(BF16)
        o = o + jnp.dot(qd, s.astype(BF16), preferred_element_type=F32)
        kd = (k.astype(F32) * kdec_ref[h]).astype(BF16)
        s_scr[h] = gc_ref[h] * s + lax.dot_general(kd, v, (((0,), (0,)), ((), ())), preferred_element_type=F32)
        y = (g_ref[:, h * B_DV:(h + 1) * B_DV].astype(F32) * _rms(o)).astype(BF16)
        part = jnp.dot(y, wp_ref[h * B_DV:(h + 1) * B_DV, :], preferred_element_type=F32)
        acc = part if acc is None else acc + part
    pb_ref[...] = acc.astype(BF16)

    @pl.when(c == pl.num_programs(1) - 1)
    def _():
        sout_ref[0] = s_scr[...]


def _ret(z_act, s0, w_proj_b_bf, n_seq, seq_len, chunk):
    n = z_act.shape[0]
    nc = seq_len // chunk
    dmask, qdec, kdec, gc = _ret_consts(chunk)
    row = lambda b, c: b * nc + c
    state_spec = pl.BlockSpec((1, B_HEADS, B_DK, B_DV), lambda b, c: (b, 0, 0, 0))
    full = lambda shape: pl.BlockSpec(shape, lambda b, c: (0,) * len(shape))
    return pl.pallas_call(
        _ret_kernel,
        grid=(n_seq, nc),
        in_specs=[
            pl.BlockSpec((chunk, B_QK), lambda b, c: (row(b, c), 2)),
            pl.BlockSpec((chunk, B_QK), lambda b, c: (row(b, c), 3)),
            pl.BlockSpec((chunk, B_V), lambda b, c: (row(b, c), 2)),
            pl.BlockSpec((chunk, B_V), lambda b, c: (row(b, c), 3)),
            state_spec,
            full((B_HEADS, chunk, chunk)),
            full((B_HEADS, chunk, 1)),
            full((B_HEADS, chunk, 1)),
            full((B_HEADS, 1, 1)),
            full((B_V, D_MODEL)),
        ],
        out_specs=[
            pl.BlockSpec((chunk, D_MODEL), lambda b, c: (row(b, c), 0)),
            state_spec,
        ],
        out_shape=[
            jax.ShapeDtypeStruct((n, D_MODEL), BF16),
            jax.ShapeDtypeStruct((n_seq, B_HEADS, B_DK, B_DV), F32),
        ],
        scratch_shapes=[pltpu.VMEM((B_HEADS, B_DK, B_DV), F32)],
        compiler_params=_cparams("parallel", "arbitrary"),
        name="ret",
    )(z_act, z_act, z_act, z_act, s0, dmask, qdec, kdec, gc, w_proj_b_bf)


def _ret_consts(chunk):
    log_g = jnp.log(1.0 - 2.0 ** (-5.0 - jnp.arange(B_HEADS, dtype=F32)))
    idx = jnp.arange(chunk, dtype=F32)
    diff = idx[:, None] - idx[None, :]
    lg = log_g[:, None]
    dmask = jnp.where(diff[None] >= 0.0, jnp.exp(lg[:, :, None] * jnp.maximum(diff, 0.0)[None]), 0.0)
    qdec = jnp.exp(lg * (idx + 1.0))[:, :, None]
    kdec = jnp.exp(lg * (chunk - 1.0 - idx))[:, :, None]
    gc = jnp.exp(log_g * chunk)[:, None, None]
    return dmask, qdec, kdec, gc


def _merge_kernel(ga_ref, gb_ref, pa_ref, pb_ref, x_ref, wo_ref, gffn_ref, wq_ref, k1_ref, k2_ref,
                  x1_ref, t_ref, s1_ref, s2_ref):
    nt = (((1,), (1,)), ((), ()))
    for c in range(x_ref.shape[0] // MERGE_ROW_CHUNK):
        rows = slice(c * MERGE_ROW_CHUNK, (c + 1) * MERGE_ROW_CHUNK)
        m = (ga_ref[rows, :].astype(F32) * pa_ref[rows, :].astype(F32)
             + gb_ref[rows, :].astype(F32) * pb_ref[rows, :].astype(F32))
        x1 = x_ref[rows, :] + jnp.dot(m.astype(BF16), wo_ref[...], preferred_element_type=F32)
        x1_ref[rows, :] = x1
        t = _rms(x1) * gffn_ref[...]
        t_ref[rows, :] = t
        qq = jnp.dot(t.astype(BF16), wq_ref[...], preferred_element_type=F32).astype(BF16)
        for h in range(P_HEADS):
            base = h * 2 * P_HALF
            s1_ref[h, :, rows] = lax.dot_general(k1_ref[...], qq[:, base:base + P_HALF], nt,
                                                 preferred_element_type=F32)
            s2_ref[h, :, rows] = lax.dot_general(k2_ref[...], qq[:, base + P_HALF:base + 2 * P_HALF], nt,
                                                 preferred_element_type=F32)


def _merge(z_act, pa, pb, x, w_out_bf, g_ffn, w_query_bf, k1_bf, k2_bf, tm):
    n = x.shape[0]
    tok = lambda i: (i, 0)
    const = lambda i: (0, 0)
    return pl.pallas_call(
        _merge_kernel,
        grid=(n // tm,),
        in_specs=[
            pl.BlockSpec((tm, D_MODEL), lambda i: (i, 8)),
            pl.BlockSpec((tm, D_MODEL), lambda i: (i, 9)),
            pl.BlockSpec((tm, D_MODEL), tok),
            pl.BlockSpec((tm, D_MODEL), tok),
            pl.BlockSpec((tm, D_MODEL), tok),
            pl.BlockSpec((D_MODEL, D_MODEL), const),
            pl.BlockSpec((1, D_MODEL), const),
            pl.BlockSpec((D_MODEL, 2 * P_HALF * P_HEADS), const),
            pl.BlockSpec((P_NKEYS, P_HALF), const),
            pl.BlockSpec((P_NKEYS, P_HALF), const),
        ],
        out_specs=[
            pl.BlockSpec((tm, D_MODEL), tok),
            pl.BlockSpec((tm, D_MODEL), tok),
            pl.BlockSpec((P_HEADS, P_NKEYS, tm), lambda i: (0, 0, i)),
            pl.BlockSpec((P_HEADS, P_NKEYS, tm), lambda i: (0, 0, i)),
        ],
        out_shape=[
            jax.ShapeDtypeStruct((n, D_MODEL), F32),
            jax.ShapeDtypeStruct((n, D_MODEL), F32),
            jax.ShapeDtypeStruct((P_HEADS, P_NKEYS, n), F32),
            jax.ShapeDtypeStruct((P_HEADS, P_NKEYS, n), F32),
        ],
        compiler_params=_cparams("parallel"),
        name="merge",
    )(z_act, z_act, pa, pb, x, w_out_bf, g_ffn.reshape(1, -1), w_query_bf, k1_bf, k2_bf)


def _sort_network(n):
    pairs = []
    p = 1
    while p < n:
        k = p
        while k >= 1:
            for j in range(k % p, n - k, 2 * k):
                for i in range(min(k, n - j - k)):
                    if (i + j) // (2 * p) == (i + j + k) // (2 * p):
                        pairs.append((i + j, i + j + k))
            k //= 2
        p *= 2
    return pairs


def _top_of_keys(scores, count):
    tl = scores.shape[1]
    n_tiles = P_NKEYS // 8
    sub = lax.broadcasted_iota(jnp.int32, (8, tl), 0).astype(F32)
    vals = [scores[8 * i:8 * (i + 1)] for i in range(n_tiles)]
    ids = [sub + float(8 * i) for i in range(n_tiles)]
    for i, j in _sort_network(n_tiles):
        swap = (vals[j] > vals[i]) | ((vals[j] == vals[i]) & (ids[j] < ids[i]))
        vals[i], vals[j] = jnp.where(swap, vals[j], vals[i]), jnp.where(swap, vals[i], vals[j])
        ids[i], ids[j] = jnp.where(swap, ids[j], ids[i]), jnp.where(swap, ids[i], ids[j])
    out_v, out_i = [], []
    for it in range(count):
        m = jnp.max(vals[0], axis=0, keepdims=True)
        sel = jnp.min(jnp.where(vals[0] == m, ids[0], float(P_NKEYS)), axis=0, keepdims=True)
        out_v.append(m)
        out_i.append(sel)
        hit = ids[0] == sel
        for d in range(count - 1 - it):
            vals[d] = jnp.where(hit, vals[d + 1], vals[d])
            ids[d] = jnp.where(hit, ids[d + 1], ids[d])
    return out_v, out_i


def _candidate_layout():
    pairs = [(a, b) for a in range(P_TOPK) for b in range(P_TOPK // (a + 1))]
    pairs += [None] * (-len(pairs) % 8)
    return [pairs[i:i + 8] for i in range(0, len(pairs), 8)]


def _topk_kernel(s1_ref, s2_ref, idx_ref, w_ref, idx_scr, w_scr):
    tl = s1_ref.shape[2]
    layout = _candidate_layout()
    big = float(P_TOPK * P_TOPK)
    row = lambda x: jnp.full((1, tl), x, F32)
    cf = jnp.concatenate([row(big if p is None else p[0] * P_TOPK + p[1]) for tile in layout for p in tile], axis=0)

    def head_body(h, carry):
        v1, i1 = _top_of_keys(s1_ref[h], P_TOPK)
        v2, i2 = _top_of_keys(s2_ref[h], P_TOPK)
        cv = jnp.concatenate([row(-jnp.inf) if p is None else v1[p[0]] + v2[p[1]]
                              for tile in layout for p in tile], axis=0)
        ce = jnp.concatenate([row(-1.0) if p is None else i1[p[0]] * P_NKEYS + i2[p[1]]
                              for tile in layout for p in tile], axis=0)
        vals, experts = [], []
        for _ in range(P_TOPK):
            m = jnp.max(cv, axis=0, keepdims=True)
            sel = jnp.min(jnp.where(cv == m, cf, big), axis=0, keepdims=True)
            hit = cf == sel
            vals.append(m)
            experts.append(jnp.max(jnp.where(hit, ce, -1.0), axis=0, keepdims=True))
            cv = jnp.where(hit, -jnp.inf, cv)
        vals = jnp.concatenate(vals, axis=0)
        e = jnp.exp(vals - vals[0:1])
        w = e / jnp.sum(e, axis=0, keepdims=True)
        rows = pl.ds(pl.multiple_of(h * P_TOPK, P_TOPK), P_TOPK)
        idx_scr[rows, :] = jnp.concatenate(experts, axis=0).astype(jnp.int32) * TABLE_ROWS_PER_EXPERT
        w_scr[rows, :] = w
        return carry

    def head_pair(g, carry):
        head_body(2 * g, carry)
        head_body(2 * g + 1, carry)
        return carry

    lax.fori_loop(0, P_HEADS // 2, head_pair, 0)
    idx_ref[...] = idx_scr[...].T
    w_ref[...] = w_scr[...].T


def _topk(s1t, s2t, tl):
    n = s1t.shape[2]
    sc = pl.BlockSpec((P_HEADS, P_NKEYS, tl), lambda i: (0, 0, i))
    tok = pl.BlockSpec((tl, P_PAIRS), lambda i: (i, 0))
    return pl.pallas_call(
        _topk_kernel,
        grid=(n // tl,),
        in_specs=[sc, sc],
        out_specs=[tok, tok],
        out_shape=[
            jax.ShapeDtypeStruct((n, P_PAIRS), jnp.int32),
            jax.ShapeDtypeStruct((n, P_PAIRS), F32),
        ],
        scratch_shapes=[pltpu.VMEM((P_PAIRS, tl), jnp.int32), pltpu.VMEM((P_PAIRS, tl), F32)],
        compiler_params=_cparams("parallel"),
        name="topk",
    )(s1t, s2t)


def _unpack_rows(x):
    lo = lax.bitcast_convert_type(x << 16, F32)
    hi = lax.bitcast_convert_type(x & jnp.uint32(0xFFFF0000), F32)
    return lo, hi


def _copy_rows(tab_ref, idx_ref, t, dst_ref):
    rpe = TABLE_ROWS_PER_EXPERT
    for k in range(P_PAIRS):
        row = pl.multiple_of(idx_ref[t, k], rpe)
        dst_ref[k * rpe:(k + 1) * rpe, :] = tab_ref[pl.ds(row, rpe), :]


def _peer_u_kernel(idx_ref, t_ref, w_ref, tab_ref, c_ref, *g_scrs):
    tb = t_ref.shape[0]
    rpe = TABLE_ROWS_PER_EXPERT
    ones = jnp.ones((8, 2 * LANES), BF16)
    contract_lanes = (((1,), (1,)), ((), ()))

    def group_body(g, carry):
        for u, g_scr in enumerate(g_scrs):
            t = g * len(g_scrs) + u
            _copy_rows(tab_ref, idx_ref, t, g_scr)
            t_row = t_ref[pl.ds(t, 1), :]
            part = None
            for r in range(rpe):
                lo, hi = _unpack_rows(g_scr[pl.ds(r, P_PAIRS, stride=rpe), :])
                t_lo = t_row[:, r * LANES:(r + 1) * LANES]
                t_hi = t_row[:, (r + rpe) * LANES:(r + rpe + 1) * LANES]
                term = lo * t_lo + hi * t_hi
                part = term if part is None else part + term
            p_hi = part.astype(BF16)
            p_lo = (part - p_hi.astype(F32)).astype(BF16)
            sums = lax.dot_general(ones, jnp.concatenate([p_hi, p_lo], axis=1), contract_lanes,
                                   preferred_element_type=F32)
            c_ref[pl.ds(t, 1), :] = sums[0:1]
        return carry

    lax.fori_loop(0, tb // len(g_scrs), group_body, 0)
    c_ref[...] = w_ref[...] * jax.nn.gelu(c_ref[...])


def _peer_u(idx_t, t, w_t, table, tb):
    n = t.shape[0]
    return pl.pallas_call(
        _peer_u_kernel,
        grid=(n // tb,),
        in_specs=[
            pl.BlockSpec((tb, P_PAIRS), lambda i: (i, 0), memory_space=pltpu.SMEM),
            pl.BlockSpec((tb, D_MODEL), lambda i: (i, 0)),
            pl.BlockSpec((tb, P_PAIRS), lambda i: (i, 0)),
            pl.BlockSpec(table.shape, lambda i: (0, 0), pipeline_mode=pl.Buffered(1)),
        ],
        out_specs=pl.BlockSpec((tb, P_PAIRS), lambda i: (i, 0)),
        out_shape=jax.ShapeDtypeStruct((n, P_PAIRS), F32),
        scratch_shapes=[pltpu.VMEM((P_PAIRS * TABLE_ROWS_PER_EXPERT, LANES), jnp.uint32)] * PEER_TOKENS_PER_ITER,
        compiler_params=_cparams("arbitrary"),
        name="peer_u",
    )(idx_t, t, w_t, table)


def _peer_v_kernel(idx_ref, c_ref, x1_ref, gfin_ref, tab_ref, y_ref, *g_scrs):
    tb = x1_ref.shape[0]
    rpe = TABLE_ROWS_PER_EXPERT

    def group_body(g, carry):
        for u, g_scr in enumerate(g_scrs):
            t = g * len(g_scrs) + u
            _copy_rows(tab_ref, idx_ref, t, g_scr)
            cmat = jnp.broadcast_to(c_ref[pl.ds(t, 1), :], (P_PAIRS, P_PAIRS)).T
            lo_rows, hi_rows = [], []
            for r in range(rpe):
                lo, hi = _unpack_rows(g_scr[pl.ds(r, P_PAIRS, stride=rpe), :])
                lo_rows.append(jnp.sum(lo * cmat, axis=0, keepdims=True))
                hi_rows.append(jnp.sum(hi * cmat, axis=0, keepdims=True))
            y_ref[pl.ds(t, 1), :] = x1_ref[pl.ds(t, 1), :] + jnp.concatenate(lo_rows + hi_rows, axis=1)
        return carry

    lax.fori_loop(0, tb // len(g_scrs), group_body, 0)
    y_ref[...] = _rms(y_ref[...]) * gfin_ref[...]


def _peer_v(idx_t, c_t, x1, g_final, table, tb):
    n = x1.shape[0]
    smem = pl.BlockSpec((tb, P_PAIRS), lambda i: (i, 0), memory_space=pltpu.SMEM)
    tok = pl.BlockSpec((tb, D_MODEL), lambda i: (i, 0))
    return pl.pallas_call(
        _peer_v_kernel,
        grid=(n // tb,),
        in_specs=[
            smem, pl.BlockSpec((tb, P_PAIRS), lambda i: (i, 0)), tok,
            pl.BlockSpec((1, D_MODEL), lambda i: (0, 0)),
            pl.BlockSpec(table.shape, lambda i: (0, 0), pipeline_mode=pl.Buffered(1)),
        ],
        out_specs=tok,
        out_shape=jax.ShapeDtypeStruct((n, D_MODEL), F32),
        scratch_shapes=[pltpu.VMEM((P_PAIRS * TABLE_ROWS_PER_EXPERT, LANES), jnp.uint32)] * PEER_TOKENS_PER_ITER,
        compiler_params=_cparams("arbitrary"),
        name="peer_v",
    )(idx_t, c_t, x1, g_final.reshape(1, D_MODEL), table)


def _pack_table(e):
    bits = lax.bitcast_convert_type(e.astype(BF16), jnp.uint16).astype(jnp.uint32)
    half = D_MODEL // 2
    words = bits[:, :half] | (bits[:, half:] << 16)
    return words.reshape(e.shape[0] * TABLE_ROWS_PER_EXPERT, LANES)


def _rope_tables(pos):
    half = B_DK // 2
    inv = 1.0 / (ROPE_BASE ** jnp.linspace(0.0, 1.0, half, dtype=F32))
    ang = pos.astype(F32)[:, None] * inv[None, :]
    return jnp.cos(ang), jnp.sin(ang)


def _layer(x, n_seq, seq_len, pos0, s0, p, tabs, g_final):
    n = x.shape[0]
    tm_in = min(n, 1024)
    rows = 256
    ac = min(seq_len, A_CHUNK)
    chunk = min(seq_len, 256)
    assert seq_len % chunk == 0 and rows % ac == 0 and n % rows == 0 and n % tm_in == 0
    assert seq_len <= RET_CHUNK or seq_len % RET_CHUNK == 0

    pos = pos0 + jnp.arange(seq_len)
    if seq_len < tm_in:
        pos = jnp.tile(pos, tm_in // seq_len)
    cos_t, sin_t = _rope_tables(pos)
    z_act, v_norm = _inproj(x, p["g_mix"], p["w_in"], p["g_sgu"], p["b_gate"], cos_t, sin_t, tm_in)

    w_tiled = jnp.tile(p["w_s"][:, :ac, :ac], (1, rows // ac, rows // ac))
    bias_full = jnp.repeat(jnp.tile(p["b_s"][:, :ac].T, (rows // ac, 1)), A_GDIM, axis=1)
    pa = _sgu(z_act, w_tiled, bias_full, p["w_proj_a"], ac, rows)
    pb, s_new = _ret(z_act, s0, p["w_proj_b"], n_seq, seq_len, chunk)

    x1, t, s1t, s2t = _merge(z_act, pa, pb, x, p["w_out"], p["g_ffn"], p["w_query"], p["k1"], p["k2"], rows)
    idx_t, w_t = _topk(s1t, s2t, LANES)
    c_t = _peer_u(idx_t, t, w_t, tabs[0], LANES)
    y = _peer_v(idx_t, c_t, x1, g_final, tabs[1], LANES)
    return y, s_new, v_norm


def kernel(x_prompt, x_sample, state_ret, w_in, w_s, b_s, g_sgu, w_proj_a, w_proj_b, b_gate, w_out, g_mix,
           g_ffn, w_query, sub_keys_1, sub_keys_2, expert_u, expert_v, g_final):
    assert w_in.shape[0] == 1, "single-layer trunk only"
    bp, lp, _ = x_prompt.shape
    bs, ls, _ = x_sample.shape
    past_len = 1024
    p = dict(
        g_mix=g_mix[0], w_in=w_in[0].astype(BF16), g_sgu=g_sgu[0], b_gate=b_gate[0],
        w_s=w_s[0], b_s=b_s[0], w_proj_a=w_proj_a[0].astype(BF16), w_proj_b=w_proj_b[0].astype(BF16),
        w_out=w_out[0].astype(BF16), g_ffn=g_ffn[0], w_query=w_query[0].astype(BF16),
        k1=sub_keys_1[0].astype(BF16), k2=sub_keys_2[0].astype(BF16))
    tabs = (_pack_table(expert_u[0]), _pack_table(expert_v[0]))
    s0p = jnp.zeros((bp, B_HEADS, B_DK, B_DV), F32)
    yp, sp, _ = _layer(x_prompt.reshape(bp * lp, D_MODEL), bp, lp, 0, s0p, p, tabs, g_final)
    ys, ss, vs = _layer(x_sample.reshape(bs * ls, D_MODEL), bs, ls, past_len, state_ret[0], p, tabs, g_final)
    return (yp.reshape(bp, lp, D_MODEL), ys.reshape(bs, ls, D_MODEL), sp[None], ss[None],
            vs.reshape(1, bs, ls, D_MODEL))
```

```python
import functools

import jax
import jax.numpy as jnp
from jax import lax
from jax.experimental import pallas as pl
from jax.experimental.pallas import tpu as pltpu

F32 = jnp.float32
BF16 = jnp.bfloat16

D_MODEL = 1024
EPS = 1e-6
A_CHUNK = 128
A_GROUPS = 8
A_GDIM = D_MODEL // A_GROUPS
B_HEADS = 4
B_DK = 256
B_DV = 512
B_QK = B_HEADS * B_DK
B_V = B_HEADS * B_DV
ROPE_BASE = 10000.0
RET_CHUNK = 64
IN_COLS = 10 * D_MODEL
P_HEADS = 8
P_NKEYS = 128
P_HALF = 128
P_TOPK = 16
P_PAIRS = P_HEADS * P_TOPK

LANES = 128
TABLE_ROWS_PER_EXPERT = D_MODEL // (2 * LANES)
INPROJ_ROW_CHUNK = 256
MERGE_ROW_CHUNK = 256
PEER_TOKENS_PER_ITER = 8
VMEM_LIMIT = 56 * 1024 * 1024


def _cparams(*sem):
    return pltpu.CompilerParams(dimension_semantics=sem, vmem_limit_bytes=VMEM_LIMIT)


def _rms(x):
    return x * lax.rsqrt(jnp.mean(x * x, axis=-1, keepdims=True) + EPS)


def _inproj_kernel(x_ref, gmix_ref, w_ref, gsgu_ref, bgate_ref, cos_ref, sin_ref, z_ref, vn_ref, h_scr):
    j = pl.program_id(1)
    tm = x_ref.shape[0]

    @pl.when(j == 0)
    def _():
        h = _rms(x_ref[...]) * gmix_ref[...]
        h_scr[...] = h.astype(BF16)

    def per_chunk(epilogue):
        for c in range(tm // INPROJ_ROW_CHUNK):
            rows = slice(c * INPROJ_ROW_CHUNK, (c + 1) * INPROJ_ROW_CHUNK)
            epilogue(jnp.dot(h_scr[rows, :], w_ref[...], preferred_element_type=F32), rows)

    def store(f):
        def epilogue(z, rows):
            z_ref[rows, :] = f(z).astype(BF16)
        return epilogue

    def store_vnorm(z, rows):
        vn = _rms(jax.nn.gelu(z)) * gsgu_ref[...]
        vn_ref[rows, :] = vn
        z_ref[rows, :] = vn.astype(BF16)

    def store_rotary(scale):
        def epilogue(z, rows):
            cos = cos_ref[rows, :]
            sin = sin_ref[rows, :]
            half = B_DK // 2
            for h in range(B_HEADS):
                x1 = z[:, h * B_DK:h * B_DK + half]
                x2 = z[:, h * B_DK + half:(h + 1) * B_DK]
                z_ref[rows, h * B_DK:h * B_DK + half] = ((x1 * cos - x2 * sin) * scale).astype(BF16)
                z_ref[rows, h * B_DK + half:(h + 1) * B_DK] = ((x1 * sin + x2 * cos) * scale).astype(BF16)
        return epilogue

    pl.when(j == 0)(lambda: per_chunk(store(jax.nn.gelu)))
    pl.when(j == 1)(lambda: per_chunk(store_vnorm))
    pl.when(j == 2)(lambda: per_chunk(store_rotary(1.0)))
    pl.when(j == 3)(lambda: per_chunk(store_rotary(B_DK ** -0.5)))
    pl.when((j == 4) | (j == 5))(lambda: per_chunk(store(lambda z: z)))
    pl.when((j == 6) | (j == 7))(lambda: per_chunk(store(jax.nn.silu)))
    pl.when(j == 8)(lambda: per_chunk(store(lambda z: jax.nn.sigmoid(z + bgate_ref[:, :D_MODEL]))))
    pl.when(j == 9)(lambda: per_chunk(store(lambda z: jax.nn.sigmoid(z + bgate_ref[:, D_MODEL:]))))


def _inproj(x, g_mix, w_in_bf, g_sgu, b_gate, cos_t, sin_t, tm):
    n = x.shape[0]
    n_pos_blocks = cos_t.shape[0] // tm
    const = lambda i, j: (0, 0)
    return pl.pallas_call(
        _inproj_kernel,
        grid=(n // tm, IN_COLS // D_MODEL),
        in_specs=[
            pl.BlockSpec((tm, D_MODEL), lambda i, j: (i, 0)),
            pl.BlockSpec((1, D_MODEL), const),
            pl.BlockSpec((D_MODEL, D_MODEL), lambda i, j: (0, j)),
            pl.BlockSpec((1, D_MODEL), const),
            pl.BlockSpec((1, 2 * D_MODEL), const),
            pl.BlockSpec((tm, B_DK // 2), lambda i, j: (i % n_pos_blocks, 0)),
            pl.BlockSpec((tm, B_DK // 2), lambda i, j: (i % n_pos_blocks, 0)),
        ],
        out_specs=[
            pl.BlockSpec((tm, D_MODEL), lambda i, j: (i, j)),
            pl.BlockSpec((tm, D_MODEL), lambda i, j: (i, 0)),
        ],
        out_shape=[
            jax.ShapeDtypeStruct((n, IN_COLS), BF16),
            jax.ShapeDtypeStruct((n, D_MODEL), F32),
        ],
        scratch_shapes=[pltpu.VMEM((tm, D_MODEL), BF16)],
        compiler_params=_cparams("parallel", "arbitrary"),
        name="inproj",
    )(x, g_mix.reshape(1, -1), w_in_bf, g_sgu.reshape(1, -1), b_gate.reshape(1, -1), cos_t, sin_t)


def _sgu_kernel(ac, u_ref, v_ref, wt_ref, bias_ref, wp_ref, pa_ref, wm_scr):
    r = u_ref.shape[0]

    @pl.when(pl.program_id(0) == 0)
    def _():
        row = lax.broadcasted_iota(jnp.int32, (r, r), 0)
        col = lax.broadcasted_iota(jnp.int32, (r, r), 1)
        keep = (row // ac == col // ac) & (col <= row)
        for g in range(A_GROUPS):
            wm_scr[g] = jnp.where(keep, wt_ref[g], 0.0).astype(BF16)

    v = v_ref[...]
    mixed = jnp.concatenate(
        [jnp.dot(wm_scr[g], v[:, g * A_GDIM:(g + 1) * A_GDIM], preferred_element_type=F32)
         for g in range(A_GROUPS)], axis=1) + bias_ref[...]
    y = (u_ref[...].astype(F32) * mixed).astype(BF16)
    pa_ref[...] = jnp.dot(y, wp_ref[...], preferred_element_type=F32).astype(BF16)


def _sgu(z_act, w_tiled, bias_full, w_proj_a_bf, ac, r):
    n = z_act.shape[0]
    return pl.pallas_call(
        functools.partial(_sgu_kernel, ac),
        grid=(n // r,),
        in_specs=[
            pl.BlockSpec((r, D_MODEL), lambda i: (i, 0)),
            pl.BlockSpec((r, D_MODEL), lambda i: (i, 1)),
            pl.BlockSpec((A_GROUPS, r, r), lambda i: (0, 0, 0)),
            pl.BlockSpec((r, D_MODEL), lambda i: (0, 0)),
            pl.BlockSpec((D_MODEL, D_MODEL), lambda i: (0, 0)),
        ],
        out_specs=pl.BlockSpec((r, D_MODEL), lambda i: (i, 0)),
        out_shape=jax.ShapeDtypeStruct((n, D_MODEL), BF16),
        scratch_shapes=[pltpu.VMEM((A_GROUPS, r, r), BF16)],
        compiler_params=_cparams("arbitrary"),
        name="sgu",
    )(z_act, z_act, w_tiled, bias_full, w_proj_a_bf)


def _ret_kernel(q_ref, k_ref, v_ref, g_ref, s0_ref, dmask_ref, qdec_ref, kdec_ref, gc_ref,
                wp_ref, pb_ref, sout_ref, s_scr):
    c = pl.program_id(1)

    @pl.when(c == 0)
    def _():
        s_scr[...] = s0_ref[0]

    acc = None
    for h in range(B_HEADS):
        q = q_ref[:, h * B_DK:(h + 1) * B_DK]
        k = k_ref[:, h * B_DK:(h + 1) * B_DK]
        v = v_ref[:, h * B_DV:(h + 1) * B_DV]
        s = s_scr[h]
        scores = lax.dot_general(q, k, (((1,), (1,)), ((), ())), preferred_element_type=F32) * dmask_ref[h]
        o = jnp.dot(scores.astype(BF16), v, preferred_element_type=F32)
        qd = (q.astype(F32) * qdec_ref[h]).astype(BF16)
        o = o + jnp.dot(qd, s.astype(BF16), preferred_element_type=F32)
        kd = (k.astype(F32) * kdec_ref[h]).astype(BF16)
        s_scr[h] = gc_ref[h] * s + lax.dot_general(kd, v, (((0,), (0,)), ((), ())), preferred_element_type=F32)
        y = (g_ref[:, h * B_DV:(h + 1) * B_DV].astype(F32) * _rms(o)).astype(BF16)
        part = jnp.dot(y, wp_ref[h * B_DV:(h + 1) * B_DV, :], preferred_element_type=F32)
        acc = part if acc is None else acc + part
    pb_ref[...] = acc.astype(BF16)

    @pl.when(c == pl.num_programs(1) - 1)
    def _():
        sout_ref[0] = s_scr[...]


def _ret(z_act, s0, w_proj_b_bf, n_seq, seq_len, chunk):
    n = z_act.shape[0]
    nc = seq_len // chunk
    dmask, qdec, kdec, gc = _ret_consts(chunk)
    row = lambda b, c: b * nc + c
    state_spec = pl.BlockSpec((1, B_HEADS, B_DK, B_DV), lambda b, c: (b, 0, 0, 0))
    full = lambda shape: pl.BlockSpec(shape, lambda b, c: (0,) * len(shape))
    return pl.pallas_call(
        _ret_kernel,
        grid=(n_seq, nc),
        in_specs=[
            pl.BlockSpec((chunk, B_QK), lambda b, c: (row(b, c), 2)),
            pl.BlockSpec((chunk, B_QK), lambda b, c: (row(b, c), 3)),
            pl.BlockSpec((chunk, B_V), lambda b, c: (row(b, c), 2)),
            pl.BlockSpec((chunk, B_V), lambda b, c: (row(b, c), 3)),
            state_spec,
            full((B_HEADS, chunk, chunk)),
            full((B_HEADS, chunk, 1)),
            full((B_HEADS, chunk, 1)),
            full((B_HEADS, 1, 1)),
            full((B_V, D_MODEL)),
        ],
        out_specs=[
            pl.BlockSpec((chunk, D_MODEL), lambda b, c: (row(b, c), 0)),
            state_spec,
        ],
        out_shape=[
            jax.ShapeDtypeStruct((n, D_MODEL), BF16),
            jax.ShapeDtypeStruct((n_seq, B_HEADS, B_DK, B_DV), F32),
        ],
        scratch_shapes=[pltpu.VMEM((B_HEADS, B_DK, B_DV), F32)],
        compiler_params=_cparams("parallel", "arbitrary"),
        name="ret",
    )(z_act, z_act, z_act, z_act, s0, dmask, qdec, kdec, gc, w_proj_b_bf)


def _ret_consts(chunk):
    log_g = jnp.log(1.0 - 2.0 ** (-5.0 - jnp.arange(B_HEADS, dtype=F32)))
    idx = jnp.arange(chunk, dtype=F32)
    diff = idx[:, None] - idx[None, :]
    lg = log_g[:, None]
    dmask = jnp.where(diff[None] >= 0.0, jnp.exp(lg[:, :, None] * jnp.maximum(diff, 0.0)[None]), 0.0)
    qdec = jnp.exp(lg * (idx + 1.0))[:, :, None]
    kdec = jnp.exp(lg * (chunk - 1.0 - idx))[:, :, None]
    gc = jnp.exp(log_g * chunk)[:, None, None]
    return dmask, qdec, kdec, gc


def _merge_kernel(ga_ref, gb_ref, pa_ref, pb_ref, x_ref, wo_ref, gffn_ref, wq_ref, k1_ref, k2_ref,
                  x1_ref, t_ref, s1_ref, s2_ref):
    nt = (((1,), (1,)), ((), ()))
    for c in range(x_ref.shape[0] // MERGE_ROW_CHUNK):
        rows = slice(c * MERGE_ROW_CHUNK, (c + 1) * MERGE_ROW_CHUNK)
        m = (ga_ref[rows, :].astype(F32) * pa_ref[rows, :].astype(F32)
             + gb_ref[rows, :].astype(F32) * pb_ref[rows, :].astype(F32))
        x1 = x_ref[rows, :] + jnp.dot(m.astype(BF16), wo_ref[...], preferred_element_type=F32)
        x1_ref[rows, :] = x1
        t = _rms(x1) * gffn_ref[...]
        t_ref[rows, :] = t
        qq = jnp.dot(t.astype(BF16), wq_ref[...], preferred_element_type=F32).astype(BF16)
        for h in range(P_HEADS):
            base = h * 2 * P_HALF
            s1_ref[h, :, rows] = lax.dot_general(k1_ref[...], qq[:, base:base + P_HALF], nt,
                                                 preferred_element_type=F32)
            s2_ref[h, :, rows] = lax.dot_general(k2_ref[...], qq[:, base + P_HALF:base + 2 * P_HALF], nt,
                                                 preferred_element_type=F32)


def _merge(z_act, pa, pb, x, w_out_bf, g_ffn, w_query_bf, k1_bf, k2_bf, tm):
    n = x.shape[0]
    tok = lambda i: (i, 0)
    const = lambda i: (0, 0)
    return pl.pallas_call(
        _merge_kernel,
        grid=(n // tm,),
        in_specs=[
            pl.BlockSpec((tm, D_MODEL), lambda i: (i, 8)),
            pl.BlockSpec((tm, D_MODEL), lambda i: (i, 9)),
            pl.BlockSpec((tm, D_MODEL), tok),
            pl.BlockSpec((tm, D_MODEL), tok),
            pl.BlockSpec((tm, D_MODEL), tok),
            pl.BlockSpec((D_MODEL, D_MODEL), const),
            pl.BlockSpec((1, D_MODEL), const),
            pl.BlockSpec((D_MODEL, 2 * P_HALF * P_HEADS), const),
            pl.BlockSpec((P_NKEYS, P_HALF), const),
            pl.BlockSpec((P_NKEYS, P_HALF), const),
        ],
        out_specs=[
            pl.BlockSpec((tm, D_MODEL), tok),
            pl.BlockSpec((tm, D_MODEL), tok),
            pl.BlockSpec((P_HEADS, P_NKEYS, tm), lambda i: (0, 0, i)),
            pl.BlockSpec((P_HEADS, P_NKEYS, tm), lambda i: (0, 0, i)),
        ],
        out_shape=[
            jax.ShapeDtypeStruct((n, D_MODEL), F32),
            jax.ShapeDtypeStruct((n, D_MODEL), F32),
            jax.ShapeDtypeStruct((P_HEADS, P_NKEYS, n), F32),
            jax.ShapeDtypeStruct((P_HEADS, P_NKEYS, n), F32),
        ],
        compiler_params=_cparams("parallel"),
        name="merge",
    )(z_act, z_act, pa, pb, x, w_out_bf, g_ffn.reshape(1, -1), w_query_bf, k1_bf, k2_bf)


def _sort_network(n):
    pairs = []
    p = 1
    while p < n:
        k = p
        while k >= 1:
            for j in range(k % p, n - k, 2 * k):
                for i in range(min(k, n - j - k)):
                    if (i + j) // (2 * p) == (i + j + k) // (2 * p):
                        pairs.append((i + j, i + j + k))
            k //= 2
        p *= 2
    return pairs


def _top_of_keys(scores, count):
    tl = scores.shape[1]
    n_tiles = P_NKEYS // 8
    sub = lax.broadcasted_iota(jnp.int32, (8, tl), 0).astype(F32)
    vals = [scores[8 * i:8 * (i + 1)] for i in range(n_tiles)]
    ids = [sub + float(8 * i) for i in range(n_tiles)]
    for i, j in _sort_network(n_tiles):
        swap = (vals[j] > vals[i]) | ((vals[j] == vals[i]) & (ids[j] < ids[i]))
        vals[i], vals[j] = jnp.where(swap, vals[j], vals[i]), jnp.where(swap, vals[i], vals[j])
        ids[i], ids[j] = jnp.where(swap, ids[j], ids[i]), jnp.where(swap, ids[i], ids[j])
    out_v, out_i = [], []
    for it in range(count):
        m = jnp.max(vals[0], axis=0, keepdims=True)
        sel = jnp.min(jnp.where(vals[0] == m, ids[0], float(P_NKEYS)), axis=0, keepdims=True)
        out_v.append(m)
        out_i.append(sel)
        hit = ids[0] == sel
        for d in range(count - 1 - it):
            vals[d] = jnp.where(hit, vals[d + 1], vals[d])
            ids[d] = jnp.where(hit, ids[d + 1], ids[d])
    return out_v, out_i


def _candidate_layout():
    pairs = [(a, b) for a in range(P_TOPK) for b in range(P_TOPK // (a + 1))]
    pairs += [None] * (-len(pairs) % 8)
    return [pairs[i:i + 8] for i in range(0, len(pairs), 8)]


def _topk_kernel(s1_ref, s2_ref, idx_ref, w_ref, idx_scr, w_scr):
    tl = s1_ref.shape[2]
    layout = _candidate_layout()
    big = float(P_TOPK * P_TOPK)
    row = lambda x: jnp.full((1, tl), x, F32)
    cf = jnp.concatenate([row(big if p is None else p[0] * P_TOPK + p[1]) for tile in layout for p in tile], axis=0)

    def head_body(h, carry):
        v1, i1 = _top_of_keys(s1_ref[h], P_TOPK)
        v2, i2 = _top_of_keys(s2_ref[h], P_TOPK)
        cv = jnp.concatenate([row(-jnp.inf) if p is None else v1[p[0]] + v2[p[1]]
                              for tile in layout for p in tile], axis=0)
        ce = jnp.concatenate([row(-1.0) if p is None else i1[p[0]] * P_NKEYS + i2[p[1]]
                              for tile in layout for p in tile], axis=0)
        vals, experts = [], []
        for _ in range(P_TOPK):
            m = jnp.max(cv, axis=0, keepdims=True)
            sel = jnp.min(jnp.where(cv == m, cf, big), axis=0, keepdims=True)
            hit = cf == sel
            vals.append(m)
            experts.append(jnp.max(jnp.where(hit, ce, -1.0), axis=0, keepdims=True))
            cv = jnp.where(hit, -jnp.inf, cv)
        vals = jnp.concatenate(vals, axis=0)
        e = jnp.exp(vals - vals[0:1])
        w = e / jnp.sum(e, axis=0, keepdims=True)
        rows = pl.ds(pl.multiple_of(h * P_TOPK, P_TOPK), P_TOPK)
        idx_scr[rows, :] = jnp.concatenate(experts, axis=0).astype(jnp.int32) * TABLE_ROWS_PER_EXPERT
        w_scr[rows, :] = w
        return carry

    def head_pair(g, carry):
        head_body(2 * g, carry)
        head_body(2 * g + 1, carry)
        return carry

    lax.fori_loop(0, P_HEADS // 2, head_pair, 0)
    idx_ref[...] = idx_scr[...].T
    w_ref[...] = w_scr[...].T


def _topk(s1t, s2t, tl):
    n = s1t.shape[2]
    sc = pl.BlockSpec((P_HEADS, P_NKEYS, tl), lambda i: (0, 0, i))
    tok = pl.BlockSpec((tl, P_PAIRS), lambda i: (i, 0))
    return pl.pallas_call(
        _topk_kernel,
        grid=(n // tl,),
        in_specs=[sc, sc],
        out_specs=[tok, tok],
        out_shape=[
            jax.ShapeDtypeStruct((n, P_PAIRS), jnp.int32),
            jax.ShapeDtypeStruct((n, P_PAIRS), F32),
        ],
        scratch_shapes=[pltpu.VMEM((P_PAIRS, tl), jnp.int32), pltpu.VMEM((P_PAIRS, tl), F32)],
        compiler_params=_cparams("parallel"),
        name="topk",
    )(s1t, s2t)


def _unpack_rows(x):
    lo = lax.bitcast_convert_type(x << 16, F32)
    hi = lax.bitcast_convert_type(x & jnp.uint32(0xFFFF0000), F32)
    return lo, hi


def _copy_rows(tab_ref, idx_ref, t, dst_ref):
    rpe = TABLE_ROWS_PER_EXPERT
    for k in range(P_PAIRS):
        row = pl.multiple_of(idx_ref[t, k], rpe)
        dst_ref[k * rpe:(k + 1) * rpe, :] = tab_ref[pl.ds(row, rpe), :]


def _pipelined_groups(tb, idx_ref, tab_ref, g_scrs, compute_token):
    grp = len(g_scrs) // 2
    n_groups = tb // grp
    bufs = (g_scrs[:grp], g_scrs[grp:])

    def copy_group(g, which):
        for u in range(grp):
            _copy_rows(tab_ref, idx_ref, g * grp + u, bufs[which][u])

    def compute_group(g, which):
        for u in range(grp):
            compute_token(g * grp + u, bufs[which][u])

    copy_group(0, 0)

    def pair_body(p, carry):
        g = 2 * p
        copy_group(g + 1, 1)
        compute_group(g, 0)
        copy_group(g + 2, 0)
        compute_group(g + 1, 1)
        return carry

    lax.fori_loop(0, n_groups // 2 - 1, pair_body, 0)
    copy_group(n_groups - 1, 1)
    compute_group(n_groups - 2, 0)
    compute_group(n_groups - 1, 1)


def _peer_u_kernel(idx_ref, t_ref, w_ref, tab_ref, c_ref, *g_scrs):
    tb = t_ref.shape[0]
    rpe = TABLE_ROWS_PER_EXPERT
    ones = jnp.ones((8, 2 * LANES), BF16)
    contract_lanes = (((1,), (1,)), ((), ()))

    def compute_token(t, g_scr):
        t_row = t_ref[pl.ds(t, 1), :]
        part = None
        for r in range(rpe):
            lo, hi = _unpack_rows(g_scr[pl.ds(r, P_PAIRS, stride=rpe), :])
            t_lo = t_row[:, r * LANES:(r + 1) * LANES]
            t_hi = t_row[:, (r + rpe) * LANES:(r + rpe + 1) * LANES]
            term = lo * t_lo + hi * t_hi
            part = term if part is None else part + term
        p_hi = part.astype(BF16)
        p_lo = (part - p_hi.astype(F32)).astype(BF16)
        sums = lax.dot_general(ones, jnp.concatenate([p_hi, p_lo], axis=1), contract_lanes,
                               preferred_element_type=F32)
        c_ref[pl.ds(t, 1), :] = sums[0:1]

    _pipelined_groups(tb, idx_ref, tab_ref, g_scrs, compute_token)
    c_ref[...] = w_ref[...] * jax.nn.gelu(c_ref[...])


def _peer_u(idx_t, t, w_t, table, tb):
    n = t.shape[0]
    return pl.pallas_call(
        _peer_u_kernel,
        grid=(n // tb,),
        in_specs=[
            pl.BlockSpec((tb, P_PAIRS), lambda i: (i, 0), memory_space=pltpu.SMEM),
            pl.BlockSpec((tb, D_MODEL), lambda i: (i, 0)),
            pl.BlockSpec((tb, P_PAIRS), lambda i: (i, 0)),
            pl.BlockSpec(table.shape, lambda i: (0, 0), pipeline_mode=pl.Buffered(1)),
        ],
        out_specs=pl.BlockSpec((tb, P_PAIRS), lambda i: (i, 0)),
        out_shape=jax.ShapeDtypeStruct((n, P_PAIRS), F32),
        scratch_shapes=[pltpu.VMEM((P_PAIRS * TABLE_ROWS_PER_EXPERT, LANES), jnp.uint32)] * (2 * PEER_TOKENS_PER_ITER),
        compiler_params=_cparams("arbitrary"),
        name="peer_u",
    )(idx_t, t, w_t, table)


def _peer_v_kernel(idx_ref, c_ref, x1_ref, gfin_ref, tab_ref, y_ref, *g_scrs):
    tb = x1_ref.shape[0]
    rpe = TABLE_ROWS_PER_EXPERT

    def compute_token(t, g_scr):
        cmat = jnp.broadcast_to(c_ref[pl.ds(t, 1), :], (P_PAIRS, P_PAIRS)).T
        lo_rows, hi_rows = [], []
        for r in range(rpe):
            lo, hi = _unpack_rows(g_scr[pl.ds(r, P_PAIRS, stride=rpe), :])
            lo_rows.append(jnp.sum(lo * cmat, axis=0, keepdims=True))
            hi_rows.append(jnp.sum(hi * cmat, axis=0, keepdims=True))
        y_ref[pl.ds(t, 1), :] = x1_ref[pl.ds(t, 1), :] + jnp.concatenate(lo_rows + hi_rows, axis=1)

    _pipelined_groups(tb, idx_ref, tab_ref, g_scrs, compute_token)
    y_ref[...] = _rms(y_ref[...]) * gfin_ref[...]


def _peer_v(idx_t, c_t, x1, g_final, table, tb):
    n = x1.shape[0]
    smem = pl.BlockSpec((tb, P_PAIRS), lambda i: (i, 0), memory_space=pltpu.SMEM)
    tok = pl.BlockSpec((tb, D_MODEL), lambda i: (i, 0))
    return pl.pallas_call(
        _peer_v_kernel,
        grid=(n // tb,),
        in_specs=[
            smem, pl.BlockSpec((tb, P_PAIRS), lambda i: (i, 0)), tok,
            pl.BlockSpec((1, D_MODEL), lambda i: (0, 0)),
            pl.BlockSpec(table.shape, lambda i: (0, 0), pipeline_mode=pl.Buffered(1)),
        ],
        out_specs=tok,
        out_shape=jax.ShapeDtypeStruct((n, D_MODEL), F32),
        scratch_shapes=[pltpu.VMEM((P_PAIRS * TABLE_ROWS_PER_EXPERT, LANES), jnp.uint32)] * (2 * PEER_TOKENS_PER_ITER),
        compiler_params=_cparams("arbitrary"),
        name="peer_v",
    )(idx_t, c_t, x1, g_final.reshape(1, D_MODEL), table)


def _pack_table(e):
    bits = lax.bitcast_convert_type(e.astype(BF16), jnp.uint16).astype(jnp.uint32)
    half = D_MODEL // 2
    words = bits[:, :half] | (bits[:, half:] << 16)
    return words.reshape(e.shape[0] * TABLE_ROWS_PER_EXPERT, LANES)


def _rope_tables(pos):
    half = B_DK // 2
    inv = 1.0 / (ROPE_BASE ** jnp.linspace(0.0, 1.0, half, dtype=F32))
    ang = pos.astype(F32)[:, None] * inv[None, :]
    return jnp.cos(ang), jnp.sin(ang)


def _layer(x, n_seq, seq_len, pos0, s0, p, tabs, g_final):
    n = x.shape[0]
    tm_in = min(n, 1024)
    rows = 256
    ac = min(seq_len, A_CHUNK)
    chunk = min(seq_len, 256)
    assert seq_len % chunk == 0 and rows % ac == 0 and n % rows == 0 and n % tm_in == 0
    assert seq_len <= RET_CHUNK or seq_len % RET_CHUNK == 0

    pos = pos0 + jnp.arange(seq_len)
    if seq_len < tm_in:
        pos = jnp.tile(pos, tm_in // seq_len)
    cos_t, sin_t = _rope_tables(pos)
    z_act, v_norm = _inproj(x, p["g_mix"], p["w_in"], p["g_sgu"], p["b_gate"], cos_t, sin_t, tm_in)

    w_tiled = jnp.tile(p["w_s"][:, :ac, :ac], (1, rows // ac, rows // ac))
    bias_full = jnp.repeat(jnp.tile(p["b_s"][:, :ac].T, (rows // ac, 1)), A_GDIM, axis=1)
    pa = _sgu(z_act, w_tiled, bias_full, p["w_proj_a"], ac, rows)
    pb, s_new = _ret(z_act, s0, p["w_proj_b"], n_seq, seq_len, chunk)

    x1, t, s1t, s2t = _merge(z_act, pa, pb, x, p["w_out"], p["g_ffn"], p["w_query"], p["k1"], p["k2"], rows)
    idx_t, w_t = _topk(s1t, s2t, LANES)
    c_t = _peer_u(idx_t, t, w_t, tabs[0], LANES)
    y = _peer_v(idx_t, c_t, x1, g_final, tabs[1], LANES)
    return y, s_new, v_norm


def kernel(x_prompt, x_sample, state_ret, w_in, w_s, b_s, g_sgu, w_proj_a, w_proj_b, b_gate, w_out, g_mix,
           g_ffn, w_query, sub_keys_1, sub_keys_2, expert_u, expert_v, g_final):
    assert w_in.shape[0] == 1, "single-layer trunk only"
    bp, lp, _ = x_prompt.shape
    bs, ls, _ = x_sample.shape
    past_len = 1024
    p = dict(
        g_mix=g_mix[0], w_in=w_in[0].astype(BF16), g_sgu=g_sgu[0], b_gate=b_gate[0],
        w_s=w_s[0], b_s=b_s[0], w_proj_a=w_proj_a[0].astype(BF16), w_proj_b=w_proj_b[0].astype(BF16),
        w_out=w_out[0].astype(BF16), g_ffn=g_ffn[0], w_query=w_query[0].astype(BF16),
        k1=sub_keys_1[0].astype(BF16), k2=sub_keys_2[0].astype(BF16))
    tabs = (_pack_table(expert_u[0]), _pack_table(expert_v[0]))
    s0p = jnp.zeros((bp, B_HEADS, B_DK, B_DV), F32)
    yp, sp, _ = _layer(x_prompt.reshape(bp * lp, D_MODEL), bp, lp, 0, s0p, p, tabs, g_final)
    ys, ss, vs = _layer(x_sample.reshape(bs * ls, D_MODEL), bs, ls, past_len, state_ret[0], p, tabs, g_final)
    return (yp.reshape(bp, lp, D_MODEL), ys.reshape(bs, ls, D_MODEL), sp[None], ss[None],
            vs.reshape(1, bs, ls, D_MODEL))
```

```python
import functools

import jax
import jax.numpy as jnp
from jax import lax
from jax.experimental import pallas as pl
from jax.experimental.pallas import tpu as pltpu

F32 = jnp.float32
BF16 = jnp.bfloat16

D_MODEL = 1024
EPS = 1e-6
A_CHUNK = 128
A_GROUPS = 8
A_GDIM = D_MODEL // A_GROUPS
B_HEADS = 4
B_DK = 256
B_DV = 512
B_QK = B_HEADS * B_DK
B_V = B_HEADS * B_DV
ROPE_BASE = 10000.0
RET_CHUNK = 64
IN_COLS = 10 * D_MODEL
P_HEADS = 8
P_NKEYS = 128
P_HALF = 128
P_TOPK = 16
P_PAIRS = P_HEADS * P_TOPK

LANES = 128
TABLE_ROWS_PER_EXPERT = D_MODEL // (2 * LANES)
INPROJ_ROW_CHUNK = 256
MERGE_ROW_CHUNK = 256
PEER_TOKENS_PER_ITER = 8
VMEM_LIMIT = 56 * 1024 * 1024


def _cparams(*sem):
    return pltpu.CompilerParams(dimension_semantics=sem, vmem_limit_bytes=VMEM_LIMIT)


def _rms(x):
    return x * lax.rsqrt(jnp.mean(x * x, axis=-1, keepdims=True) + EPS)


def _inproj_kernel(x_ref, gmix_ref, w_ref, gsgu_ref, bgate_ref, cos_ref, sin_ref, z_ref, *rest):
    vn_ref, h_scr = rest if len(rest) == 2 else (None, rest[0])
    j = pl.program_id(1)
    tm = x_ref.shape[0]

    @pl.when(j == 0)
    def _():
        h = _rms(x_ref[...]) * gmix_ref[...]
        h_scr[...] = h.astype(BF16)

    def per_chunk(epilogue):
        for c in range(tm // INPROJ_ROW_CHUNK):
            rows = slice(c * INPROJ_ROW_CHUNK, (c + 1) * INPROJ_ROW_CHUNK)
            epilogue(jnp.dot(h_scr[rows, :], w_ref[...], preferred_element_type=F32), rows)

    def store(f):
        def epilogue(z, rows):
            z_ref[rows, :] = f(z).astype(BF16)
        return epilogue

    def store_vnorm(z, rows):
        vn = _rms(jax.nn.gelu(z)) * gsgu_ref[...]
        if vn_ref is not None:
            vn_ref[rows, :] = vn
        z_ref[rows, :] = vn.astype(BF16)

    def store_rotary(scale):
        def epilogue(z, rows):
            cos = cos_ref[rows, :]
            sin = sin_ref[rows, :]
            half = B_DK // 2
            for h in range(B_HEADS):
                x1 = z[:, h * B_DK:h * B_DK + half]
                x2 = z[:, h * B_DK + half:(h + 1) * B_DK]
                z_ref[rows, h * B_DK:h * B_DK + half] = ((x1 * cos - x2 * sin) * scale).astype(BF16)
                z_ref[rows, h * B_DK + half:(h + 1) * B_DK] = ((x1 * sin + x2 * cos) * scale).astype(BF16)
        return epilogue

    pl.when(j == 0)(lambda: per_chunk(store(jax.nn.gelu)))
    pl.when(j == 1)(lambda: per_chunk(store_vnorm))
    pl.when(j == 2)(lambda: per_chunk(store_rotary(1.0)))
    pl.when(j == 3)(lambda: per_chunk(store_rotary(B_DK ** -0.5)))
    pl.when((j == 4) | (j == 5))(lambda: per_chunk(store(lambda z: z)))
    pl.when((j == 6) | (j == 7))(lambda: per_chunk(store(jax.nn.silu)))
    pl.when(j == 8)(lambda: per_chunk(store(lambda z: jax.nn.sigmoid(z + bgate_ref[:, :D_MODEL]))))
    pl.when(j == 9)(lambda: per_chunk(store(lambda z: jax.nn.sigmoid(z + bgate_ref[:, D_MODEL:]))))


def _inproj(x, g_mix, w_in_bf, g_sgu, b_gate, cos_t, sin_t, tm, want_vnorm):
    n = x.shape[0]
    n_pos_blocks = cos_t.shape[0] // tm
    const = lambda i, j: (0, 0)
    n_out = 2 if want_vnorm else 1
    return pl.pallas_call(
        _inproj_kernel,
        grid=(n // tm, IN_COLS // D_MODEL),
        in_specs=[
            pl.BlockSpec((tm, D_MODEL), lambda i, j: (i, 0)),
            pl.BlockSpec((1, D_MODEL), const),
            pl.BlockSpec((D_MODEL, D_MODEL), lambda i, j: (0, j)),
            pl.BlockSpec((1, D_MODEL), const),
            pl.BlockSpec((1, 2 * D_MODEL), const),
            pl.BlockSpec((tm, B_DK // 2), lambda i, j: (i % n_pos_blocks, 0)),
            pl.BlockSpec((tm, B_DK // 2), lambda i, j: (i % n_pos_blocks, 0)),
        ],
        out_specs=[
            pl.BlockSpec((tm, D_MODEL), lambda i, j: (i, j)),
            pl.BlockSpec((tm, D_MODEL), lambda i, j: (i, 0)),
        ][:n_out],
        out_shape=[
            jax.ShapeDtypeStruct((n, IN_COLS), BF16),
            jax.ShapeDtypeStruct((n, D_MODEL), F32),
        ][:n_out],
        scratch_shapes=[pltpu.VMEM((tm, D_MODEL), BF16)],
        compiler_params=_cparams("parallel", "arbitrary"),
        name="inproj",
    )(x, g_mix.reshape(1, -1), w_in_bf, g_sgu.reshape(1, -1), b_gate.reshape(1, -1), cos_t, sin_t)


def _sgu_kernel(ac, u_ref, v_ref, wt_ref, bias_ref, wp_ref, pa_ref, wm_scr):
    r = u_ref.shape[0]

    @pl.when(pl.program_id(0) == 0)
    def _():
        row = lax.broadcasted_iota(jnp.int32, (r, r), 0)
        col = lax.broadcasted_iota(jnp.int32, (r, r), 1)
        keep = (row // ac == col // ac) & (col <= row)
        for g in range(A_GROUPS):
            wm_scr[g] = jnp.where(keep, wt_ref[g], 0.0).astype(BF16)

    v = v_ref[...]
    mixed = jnp.concatenate(
        [jnp.dot(wm_scr[g], v[:, g * A_GDIM:(g + 1) * A_GDIM], preferred_element_type=F32)
         for g in range(A_GROUPS)], axis=1) + bias_ref[...]
    y = (u_ref[...].astype(F32) * mixed).astype(BF16)
    pa_ref[...] = jnp.dot(y, wp_ref[...], preferred_element_type=F32).astype(BF16)


def _sgu(z_act, w_tiled, bias_full, w_proj_a_bf, ac, r):
    n = z_act.shape[0]
    return pl.pallas_call(
        functools.partial(_sgu_kernel, ac),
        grid=(n // r,),
        in_specs=[
            pl.BlockSpec((r, D_MODEL), lambda i: (i, 0)),
            pl.BlockSpec((r, D_MODEL), lambda i: (i, 1)),
            pl.BlockSpec((A_GROUPS, r, r), lambda i: (0, 0, 0)),
            pl.BlockSpec((r, D_MODEL), lambda i: (0, 0)),
            pl.BlockSpec((D_MODEL, D_MODEL), lambda i: (0, 0)),
        ],
        out_specs=pl.BlockSpec((r, D_MODEL), lambda i: (i, 0)),
        out_shape=jax.ShapeDtypeStruct((n, D_MODEL), BF16),
        scratch_shapes=[pltpu.VMEM((A_GROUPS, r, r), BF16)],
        compiler_params=_cparams("arbitrary"),
        name="sgu",
    )(z_act, z_act, w_tiled, bias_full, w_proj_a_bf)


def _ret_kernel(q_ref, k_ref, v_ref, g_ref, s0_ref, dmask_ref, qdec_ref, kdec_ref, gc_ref,
                wp_ref, pb_ref, sout_ref, s_scr):
    c = pl.program_id(1)

    @pl.when(c == 0)
    def _():
        s_scr[...] = s0_ref[0, 0]

    acc = None
    for h in range(B_HEADS):
        q = q_ref[:, h * B_DK:(h + 1) * B_DK]
        k = k_ref[:, h * B_DK:(h + 1) * B_DK]
        v = v_ref[:, h * B_DV:(h + 1) * B_DV]
        s = s_scr[h]
        scores = lax.dot_general(q, k, (((1,), (1,)), ((), ())), preferred_element_type=F32) * dmask_ref[h]
        o = jnp.dot(scores.astype(BF16), v, preferred_element_type=F32)
        qd = (q.astype(F32) * qdec_ref[h]).astype(BF16)
        o = o + jnp.dot(qd, s.astype(BF16), preferred_element_type=F32)
        kd = (k.astype(F32) * kdec_ref[h]).astype(BF16)
        s_scr[h] = gc_ref[h] * s + lax.dot_general(kd, v, (((0,), (0,)), ((), ())), preferred_element_type=F32)
        y = (g_ref[:, h * B_DV:(h + 1) * B_DV].astype(F32) * _rms(o)).astype(BF16)
        part = jnp.dot(y, wp_ref[h * B_DV:(h + 1) * B_DV, :], preferred_element_type=F32)
        acc = part if acc is None else acc + part
    pb_ref[...] = acc.astype(BF16)

    @pl.when(c == pl.num_programs(1) - 1)
    def _():
        sout_ref[0, 0] = s_scr[...]


def _ret(z_act, s0, w_proj_b_bf, n_seq, seq_len, chunk):
    n = z_act.shape[0]
    nc = seq_len // chunk
    dmask, qdec, kdec, gc = _ret_consts(chunk)
    row = lambda b, c: b * nc + c
    state_spec = pl.BlockSpec((1, 1, B_HEADS, B_DK, B_DV), lambda b, c: (0, b, 0, 0, 0))
    full = lambda shape: pl.BlockSpec(shape, lambda b, c: (0,) * len(shape))
    return pl.pallas_call(
        _ret_kernel,
        grid=(n_seq, nc),
        in_specs=[
            pl.BlockSpec((chunk, B_QK), lambda b, c: (row(b, c), 2)),
            pl.BlockSpec((chunk, B_QK), lambda b, c: (row(b, c), 3)),
            pl.BlockSpec((chunk, B_V), lambda b, c: (row(b, c), 2)),
            pl.BlockSpec((chunk, B_V), lambda b, c: (row(b, c), 3)),
            state_spec,
            full((B_HEADS, chunk, chunk)),
            full((B_HEADS, chunk, 1)),
            full((B_HEADS, chunk, 1)),
            full((B_HEADS, 1, 1)),
            full((B_V, D_MODEL)),
        ],
        out_specs=[
            pl.BlockSpec((chunk, D_MODEL), lambda b, c: (row(b, c), 0)),
            state_spec,
        ],
        out_shape=[
            jax.ShapeDtypeStruct((n, D_MODEL), BF16),
            jax.ShapeDtypeStruct((1, n_seq, B_HEADS, B_DK, B_DV), F32),
        ],
        scratch_shapes=[pltpu.VMEM((B_HEADS, B_DK, B_DV), F32)],
        compiler_params=_cparams("parallel", "arbitrary"),
        name="ret",
    )(z_act, z_act, z_act, z_act, s0, dmask, qdec, kdec, gc, w_proj_b_bf)


def _ret_consts(chunk):
    log_g = jnp.log(1.0 - 2.0 ** (-5.0 - jnp.arange(B_HEADS, dtype=F32)))
    idx = jnp.arange(chunk, dtype=F32)
    diff = idx[:, None] - idx[None, :]
    lg = log_g[:, None]
    dmask = jnp.where(diff[None] >= 0.0, jnp.exp(lg[:, :, None] * jnp.maximum(diff, 0.0)[None]), 0.0)
    qdec = jnp.exp(lg * (idx + 1.0))[:, :, None]
    kdec = jnp.exp(lg * (chunk - 1.0 - idx))[:, :, None]
    gc = jnp.exp(log_g * chunk)[:, None, None]
    return dmask, qdec, kdec, gc


def _merge_kernel(ga_ref, gb_ref, pa_ref, pb_ref, x_ref, wo_ref, gffn_ref, wq_ref, k1_ref, k2_ref,
                  x1_ref, t_ref, s1_ref, s2_ref):
    nt = (((1,), (1,)), ((), ()))
    for c in range(x_ref.shape[0] // MERGE_ROW_CHUNK):
        rows = slice(c * MERGE_ROW_CHUNK, (c + 1) * MERGE_ROW_CHUNK)
        m = (ga_ref[rows, :].astype(F32) * pa_ref[rows, :].astype(F32)
             + gb_ref[rows, :].astype(F32) * pb_ref[rows, :].astype(F32))
        x1 = x_ref[rows, :] + jnp.dot(m.astype(BF16), wo_ref[...], preferred_element_type=F32)
        x1_ref[rows, :] = x1
        t = _rms(x1) * gffn_ref[...]
        t_ref[rows, :] = t
        qq = jnp.dot(t.astype(BF16), wq_ref[...], preferred_element_type=F32).astype(BF16)
        for h in range(P_HEADS):
            base = h * 2 * P_HALF
            s1_ref[h, :, rows] = lax.dot_general(k1_ref[...], qq[:, base:base + P_HALF], nt,
                                                 preferred_element_type=F32)
            s2_ref[h, :, rows] = lax.dot_general(k2_ref[...], qq[:, base + P_HALF:base + 2 * P_HALF], nt,
                                                 preferred_element_type=F32)


def _merge(z_act, pa, pb, x, w_out_bf, g_ffn, w_query_bf, k1_bf, k2_bf, tm):
    n = x.shape[0]
    tok = lambda i: (i, 0)
    const = lambda i: (0, 0)
    return pl.pallas_call(
        _merge_kernel,
        grid=(n // tm,),
        in_specs=[
            pl.BlockSpec((tm, D_MODEL), lambda i: (i, 8)),
            pl.BlockSpec((tm, D_MODEL), lambda i: (i, 9)),
            pl.BlockSpec((tm, D_MODEL), tok),
            pl.BlockSpec((tm, D_MODEL), tok),
            pl.BlockSpec((tm, D_MODEL), tok),
            pl.BlockSpec((D_MODEL, D_MODEL), const),
            pl.BlockSpec((1, D_MODEL), const),
            pl.BlockSpec((D_MODEL, 2 * P_HALF * P_HEADS), const),
            pl.BlockSpec((P_NKEYS, P_HALF), const),
            pl.BlockSpec((P_NKEYS, P_HALF), const),
        ],
        out_specs=[
            pl.BlockSpec((tm, D_MODEL), tok),
            pl.BlockSpec((tm, D_MODEL), tok),
            pl.BlockSpec((P_HEADS, P_NKEYS, tm), lambda i: (0, 0, i)),
            pl.BlockSpec((P_HEADS, P_NKEYS, tm), lambda i: (0, 0, i)),
        ],
        out_shape=[
            jax.ShapeDtypeStruct((n, D_MODEL), F32),
            jax.ShapeDtypeStruct((n, D_MODEL), F32),
            jax.ShapeDtypeStruct((P_HEADS, P_NKEYS, n), F32),
            jax.ShapeDtypeStruct((P_HEADS, P_NKEYS, n), F32),
        ],
        compiler_params=_cparams("parallel"),
        name="merge",
    )(z_act, z_act, pa, pb, x, w_out_bf, g_ffn.reshape(1, -1), w_query_bf, k1_bf, k2_bf)


def _sort_network(n):
    pairs = []
    p = 1
    while p < n:
        k = p
        while k >= 1:
            for j in range(k % p, n - k, 2 * k):
                for i in range(min(k, n - j - k)):
                    if (i + j) // (2 * p) == (i + j + k) // (2 * p):
                        pairs.append((i + j, i + j + k))
            k //= 2
        p *= 2
    return pairs


def _top_of_keys(scores, count):
    tl = scores.shape[1]
    n_tiles = P_NKEYS // 8
    sub = lax.broadcasted_iota(jnp.int32, (8, tl), 0).astype(F32)
    vals = [scores[8 * i:8 * (i + 1)] for i in range(n_tiles)]
    ids = [sub + float(8 * i) for i in range(n_tiles)]
    for i, j in _sort_network(n_tiles):
        swap = (vals[j] > vals[i]) | ((vals[j] == vals[i]) & (ids[j] < ids[i]))
        vals[i], vals[j] = jnp.where(swap, vals[j], vals[i]), jnp.where(swap, vals[i], vals[j])
        ids[i], ids[j] = jnp.where(swap, ids[j], ids[i]), jnp.where(swap, ids[i], ids[j])
    out_v, out_i = [], []
    for it in range(count):
        m = jnp.max(vals[0], axis=0, keepdims=True)
        sel = jnp.min(jnp.where(vals[0] == m, ids[0], float(P_NKEYS)), axis=0, keepdims=True)
        out_v.append(m)
        out_i.append(sel)
        hit = ids[0] == sel
        for d in range(count - 1 - it):
            vals[d] = jnp.where(hit, vals[d + 1], vals[d])
            ids[d] = jnp.where(hit, ids[d + 1], ids[d])
    return out_v, out_i


def _candidate_layout():
    pairs = [(a, b) for a in range(P_TOPK) for b in range(P_TOPK // (a + 1))]
    pairs += [None] * (-len(pairs) % 8)
    return [pairs[i:i + 8] for i in range(0, len(pairs), 8)]


def _topk_kernel(s1_ref, s2_ref, idx_ref, w_ref, idx_scr, w_scr):
    tl = s1_ref.shape[2]
    layout = _candidate_layout()
    big = float(P_TOPK * P_TOPK)
    row = lambda x: jnp.full((1, tl), x, F32)
    cf = jnp.concatenate([row(big if p is None else p[0] * P_TOPK + p[1]) for tile in layout for p in tile], axis=0)

    def head_body(h, carry):
        v1, i1 = _top_of_keys(s1_ref[h], P_TOPK)
        v2, i2 = _top_of_keys(s2_ref[h], P_TOPK)
        cv = jnp.concatenate([row(-jnp.inf) if p is None else v1[p[0]] + v2[p[1]]
                              for tile in layout for p in tile], axis=0)
        ce = jnp.concatenate([row(-1.0) if p is None else i1[p[0]] * P_NKEYS + i2[p[1]]
                              for tile in layout for p in tile], axis=0)
        vals, experts = [], []
        for _ in range(P_TOPK):
            m = jnp.max(cv, axis=0, keepdims=True)
            sel = jnp.min(jnp.where(cv == m, cf, big), axis=0, keepdims=True)
            hit = cf == sel
            vals.append(m)
            experts.append(jnp.max(jnp.where(hit, ce, -1.0), axis=0, keepdims=True))
            cv = jnp.where(hit, -jnp.inf, cv)
        vals = jnp.concatenate(vals, axis=0)
        e = jnp.exp(vals - vals[0:1])
        w = e / jnp.sum(e, axis=0, keepdims=True)
        rows = pl.ds(pl.multiple_of(h * P_TOPK, P_TOPK), P_TOPK)
        idx_scr[rows, :] = jnp.concatenate(experts, axis=0).astype(jnp.int32) * TABLE_ROWS_PER_EXPERT
        w_scr[rows, :] = w
        return carry

    def head_pair(g, carry):
        head_body(2 * g, carry)
        head_body(2 * g + 1, carry)
        return carry

    lax.fori_loop(0, P_HEADS // 2, head_pair, 0)
    idx_ref[...] = idx_scr[...].T
    w_ref[...] = w_scr[...].T


def _topk(s1t, s2t, tl):
    n = s1t.shape[2]
    sc = pl.BlockSpec((P_HEADS, P_NKEYS, tl), lambda i: (0, 0, i))
    tok = pl.BlockSpec((tl, P_PAIRS), lambda i: (i, 0))
    return pl.pallas_call(
        _topk_kernel,
        grid=(n // tl,),
        in_specs=[sc, sc],
        out_specs=[tok, tok],
        out_shape=[
            jax.ShapeDtypeStruct((n, P_PAIRS), jnp.int32),
            jax.ShapeDtypeStruct((n, P_PAIRS), F32),
        ],
        scratch_shapes=[pltpu.VMEM((P_PAIRS, tl), jnp.int32), pltpu.VMEM((P_PAIRS, tl), F32)],
        compiler_params=_cparams("parallel"),
        name="topk",
    )(s1t, s2t)


def _unpack_rows(x):
    lo = lax.bitcast_convert_type(x << 16, F32)
    hi = lax.bitcast_convert_type(x & jnp.uint32(0xFFFF0000), F32)
    return lo, hi


def _copy_rows(tab_ref, idx_ref, t, dst_ref):
    rpe = TABLE_ROWS_PER_EXPERT
    for k in range(P_PAIRS):
        row = pl.multiple_of(idx_ref[t, k], rpe)
        dst_ref[k * rpe:(k + 1) * rpe, :] = tab_ref[pl.ds(row, rpe), :]


def _pipelined_groups(tb, idx_ref, tab_ref, g_scrs, compute_token):
    grp = len(g_scrs) // 2
    n_groups = tb // grp
    bufs = (g_scrs[:grp], g_scrs[grp:])

    def copy_group(g, which):
        for u in range(grp):
            _copy_rows(tab_ref, idx_ref, g * grp + u, bufs[which][u])

    def compute_group(g, which):
        for u in range(grp):
            compute_token(g * grp + u, bufs[which][u])

    copy_group(0, 0)

    def pair_body(p, carry):
        g = 2 * p
        copy_group(g + 1, 1)
        compute_group(g, 0)
        copy_group(g + 2, 0)
        compute_group(g + 1, 1)
        return carry

    lax.fori_loop(0, n_groups // 2 - 1, pair_body, 0)
    copy_group(n_groups - 1, 1)
    compute_group(n_groups - 2, 0)
    compute_group(n_groups - 1, 1)


def _peer_u_kernel(idx_ref, t_ref, w_ref, tab_ref, c_ref, *g_scrs):
    tb = t_ref.shape[0]
    rpe = TABLE_ROWS_PER_EXPERT
    ones = jnp.ones((8, 2 * LANES), BF16)
    contract_lanes = (((1,), (1,)), ((), ()))

    def compute_token(t, g_scr):
        t_row = t_ref[pl.ds(t, 1), :]
        part = None
        for r in range(rpe):
            lo, hi = _unpack_rows(g_scr[pl.ds(r, P_PAIRS, stride=rpe), :])
            t_lo = t_row[:, r * LANES:(r + 1) * LANES]
            t_hi = t_row[:, (r + rpe) * LANES:(r + rpe + 1) * LANES]
            term = lo * t_lo + hi * t_hi
            part = term if part is None else part + term
        p_hi = part.astype(BF16)
        p_lo = (part - p_hi.astype(F32)).astype(BF16)
        sums = lax.dot_general(ones, jnp.concatenate([p_hi, p_lo], axis=1), contract_lanes,
                               preferred_element_type=F32)
        c_ref[pl.ds(t, 1), :] = sums[0:1]

    _pipelined_groups(tb, idx_ref, tab_ref, g_scrs, compute_token)
    c_ref[...] = w_ref[...] * jax.nn.gelu(c_ref[...])


def _peer_u(idx_t, t, w_t, table, tb):
    n = t.shape[0]
    return pl.pallas_call(
        _peer_u_kernel,
        grid=(n // tb,),
        in_specs=[
            pl.BlockSpec((tb, P_PAIRS), lambda i: (i, 0), memory_space=pltpu.SMEM),
            pl.BlockSpec((tb, D_MODEL), lambda i: (i, 0)),
            pl.BlockSpec((tb, P_PAIRS), lambda i: (i, 0)),
            pl.BlockSpec(table.shape, lambda i: (0, 0), pipeline_mode=pl.Buffered(1)),
        ],
        out_specs=pl.BlockSpec((tb, P_PAIRS), lambda i: (i, 0)),
        out_shape=jax.ShapeDtypeStruct((n, P_PAIRS), F32),
        scratch_shapes=[pltpu.VMEM((P_PAIRS * TABLE_ROWS_PER_EXPERT, LANES), jnp.uint32)] * (2 * PEER_TOKENS_PER_ITER),
        compiler_params=_cparams("arbitrary"),
        name="peer_u",
    )(idx_t, t, w_t, table)


def _peer_v_kernel(idx_ref, c_ref, x1_ref, gfin_ref, tab_ref, y_ref, *g_scrs):
    tb = x1_ref.shape[0]
    rpe = TABLE_ROWS_PER_EXPERT

    def compute_token(t, g_scr):
        cmat = jnp.broadcast_to(c_ref[pl.ds(t, 1), :], (P_PAIRS, P_PAIRS)).T
        lo_rows, hi_rows = [], []
        for r in range(rpe):
            lo, hi = _unpack_rows(g_scr[pl.ds(r, P_PAIRS, stride=rpe), :])
            lo_rows.append(jnp.sum(lo * cmat, axis=0, keepdims=True))
            hi_rows.append(jnp.sum(hi * cmat, axis=0, keepdims=True))
        y_ref[pl.ds(t, 1), :] = x1_ref[pl.ds(t, 1), :] + jnp.concatenate(lo_rows + hi_rows, axis=1)

    _pipelined_groups(tb, idx_ref, tab_ref, g_scrs, compute_token)
    y_ref[...] = _rms(y_ref[...]) * gfin_ref[...]


def _peer_v(idx_t, c_t, x1, g_final, table, tb):
    n = x1.shape[0]
    smem = pl.BlockSpec((tb, P_PAIRS), lambda i: (i, 0), memory_space=pltpu.SMEM)
    tok = pl.BlockSpec((tb, D_MODEL), lambda i: (i, 0))
    return pl.pallas_call(
        _peer_v_kernel,
        grid=(n // tb,),
        in_specs=[
            smem, pl.BlockSpec((tb, P_PAIRS), lambda i: (i, 0)), tok,
            pl.BlockSpec((1, D_MODEL), lambda i: (0, 0)),
            pl.BlockSpec(table.shape, lambda i: (0, 0), pipeline_mode=pl.Buffered(1)),
        ],
        out_specs=tok,
        out_shape=jax.ShapeDtypeStruct((n, D_MODEL), F32),
        scratch_shapes=[pltpu.VMEM((P_PAIRS * TABLE_ROWS_PER_EXPERT, LANES), jnp.uint32)] * (2 * PEER_TOKENS_PER_ITER),
        compiler_params=_cparams("arbitrary"),
        name="peer_v",
    )(idx_t, c_t, x1, g_final.reshape(1, D_MODEL), table)


def _pack_table(e):
    bits = lax.bitcast_convert_type(e.astype(BF16), jnp.uint16).astype(jnp.uint32)
    half = D_MODEL // 2
    words = bits[:, :half] | (bits[:, half:] << 16)
    return words.reshape(e.shape[0] * TABLE_ROWS_PER_EXPERT, LANES)


def _rope_tables(pos):
    half = B_DK // 2
    inv = 1.0 / (ROPE_BASE ** jnp.linspace(0.0, 1.0, half, dtype=F32))
    ang = pos.astype(F32)[:, None] * inv[None, :]
    return jnp.cos(ang), jnp.sin(ang)


def _layer(x, n_seq, seq_len, pos0, s0, p, tabs, g_final, want_vnorm):
    n = x.shape[0]
    tm_in = min(n, 2048)
    rows = 256
    ac = min(seq_len, A_CHUNK)
    chunk = min(seq_len, 256)
    assert seq_len % chunk == 0 and rows % ac == 0 and n % rows == 0 and n % tm_in == 0
    assert seq_len <= RET_CHUNK or seq_len % RET_CHUNK == 0

    pos = pos0 + jnp.arange(seq_len)
    if seq_len < tm_in:
        pos = jnp.tile(pos, tm_in // seq_len)
    cos_t, sin_t = _rope_tables(pos)
    z_act, *v_norm = _inproj(x, p["g_mix"], p["w_in"], p["g_sgu"], p["b_gate"], cos_t, sin_t, tm_in, want_vnorm)

    w_tiled = jnp.tile(p["w_s"][:, :ac, :ac], (1, rows // ac, rows // ac))
    bias_full = jnp.repeat(jnp.tile(p["b_s"][:, :ac].T, (rows // ac, 1)), A_GDIM, axis=1)
    pa = _sgu(z_act, w_tiled, bias_full, p["w_proj_a"], ac, rows)
    pb, s_new = _ret(z_act, s0, p["w_proj_b"], n_seq, seq_len, chunk)

    x1, t, s1t, s2t = _merge(z_act, pa, pb, x, p["w_out"], p["g_ffn"], p["w_query"], p["k1"], p["k2"], rows)
    idx_t, w_t = _topk(s1t, s2t, LANES)
    c_t = _peer_u(idx_t, t, w_t, tabs[0], LANES)
    y = _peer_v(idx_t, c_t, x1, g_final, tabs[1], LANES)
    return y, s_new, (v_norm[0] if want_vnorm else None)


def kernel(x_prompt, x_sample, state_ret, w_in, w_s, b_s, g_sgu, w_proj_a, w_proj_b, b_gate, w_out, g_mix,
           g_ffn, w_query, sub_keys_1, sub_keys_2, expert_u, expert_v, g_final):
    assert w_in.shape[0] == 1, "single-layer trunk only"
    bp, lp, _ = x_prompt.shape
    bs, ls, _ = x_sample.shape
    past_len = 1024
    p = dict(
        g_mix=g_mix[0], w_in=w_in[0].astype(BF16), g_sgu=g_sgu[0], b_gate=b_gate[0],
        w_s=w_s[0], b_s=b_s[0], w_proj_a=w_proj_a[0].astype(BF16), w_proj_b=w_proj_b[0].astype(BF16),
        w_out=w_out[0].astype(BF16), g_ffn=g_ffn[0], w_query=w_query[0].astype(BF16),
        k1=sub_keys_1[0].astype(BF16), k2=sub_keys_2[0].astype(BF16))
    tabs = (_pack_table(expert_u[0]), _pack_table(expert_v[0]))
    s0p = jnp.zeros((1, bp, B_HEADS, B_DK, B_DV), F32)
    yp, sp, _ = _layer(x_prompt.reshape(bp * lp, D_MODEL), bp, lp, 0, s0p, p, tabs, g_final, False)
    ys, ss, vs = _layer(x_sample.reshape(bs * ls, D_MODEL), bs, ls, past_len, state_ret, p, tabs, g_final, True)
    return (yp.reshape(bp, lp, D_MODEL), ys.reshape(bs, ls, D_MODEL), sp, ss, vs.reshape(1, bs, ls, D_MODEL))
```

```python
import functools

import jax
import jax.numpy as jnp
from jax import lax
from jax.experimental import pallas as pl
from jax.experimental.pallas import tpu as pltpu

F32 = jnp.float32
BF16 = jnp.bfloat16

D_MODEL = 1024
EPS = 1e-6
A_CHUNK = 128
A_GROUPS = 8
A_GDIM = D_MODEL // A_GROUPS
B_HEADS = 4
B_DK = 256
B_DV = 512
B_QK = B_HEADS * B_DK
B_V = B_HEADS * B_DV
ROPE_BASE = 10000.0
RET_CHUNK = 64
IN_COLS = 10 * D_MODEL
P_HEADS = 8
P_NKEYS = 128
P_HALF = 128
P_TOPK = 16
P_PAIRS = P_HEADS * P_TOPK

LANES = 128
TABLE_ROWS_PER_EXPERT = D_MODEL // (2 * LANES)
INPROJ_ROW_CHUNK = 256
MERGE_ROW_CHUNK = 256
PEER_TOKENS_PER_ITER = 8
VMEM_LIMIT = 56 * 1024 * 1024


def _cparams(*sem):
    return pltpu.CompilerParams(dimension_semantics=sem, vmem_limit_bytes=VMEM_LIMIT)


def _rms(x):
    return x * lax.rsqrt(jnp.mean(x * x, axis=-1, keepdims=True) + EPS)


def _inproj_kernel(x_ref, gmix_ref, w_ref, gsgu_ref, bgate_ref, cos_ref, sin_ref, z_ref, *rest):
    vn_ref, h_scr = rest if len(rest) == 2 else (None, rest[0])
    j = pl.program_id(1)
    tm = x_ref.shape[0]

    @pl.when(j == 0)
    def _():
        h = _rms(x_ref[...]) * gmix_ref[...]
        h_scr[...] = h.astype(BF16)

    def per_chunk(epilogue):
        for c in range(tm // INPROJ_ROW_CHUNK):
            rows = slice(c * INPROJ_ROW_CHUNK, (c + 1) * INPROJ_ROW_CHUNK)
            epilogue(jnp.dot(h_scr[rows, :], w_ref[...], preferred_element_type=F32), rows)

    def store(f):
        def epilogue(z, rows):
            z_ref[rows, :] = f(z).astype(BF16)
        return epilogue

    def store_vnorm(z, rows):
        vn = _rms(jax.nn.gelu(z)) * gsgu_ref[...]
        if vn_ref is not None:
            vn_ref[rows, :] = vn
        z_ref[rows, :] = vn.astype(BF16)

    def store_rotary(scale):
        def epilogue(z, rows):
            cos = cos_ref[rows, :]
            sin = sin_ref[rows, :]
            half = B_DK // 2
            for h in range(B_HEADS):
                x1 = z[:, h * B_DK:h * B_DK + half]
                x2 = z[:, h * B_DK + half:(h + 1) * B_DK]
                z_ref[rows, h * B_DK:h * B_DK + half] = ((x1 * cos - x2 * sin) * scale).astype(BF16)
                z_ref[rows, h * B_DK + half:(h + 1) * B_DK] = ((x1 * sin + x2 * cos) * scale).astype(BF16)
        return epilogue

    pl.when(j == 0)(lambda: per_chunk(store(jax.nn.gelu)))
    pl.when(j == 1)(lambda: per_chunk(store_vnorm))
    pl.when(j == 2)(lambda: per_chunk(store_rotary(1.0)))
    pl.when(j == 3)(lambda: per_chunk(store_rotary(B_DK ** -0.5)))
    pl.when((j == 4) | (j == 5))(lambda: per_chunk(store(lambda z: z)))
    pl.when((j == 6) | (j == 7))(lambda: per_chunk(store(jax.nn.silu)))
    pl.when(j == 8)(lambda: per_chunk(store(lambda z: jax.nn.sigmoid(z + bgate_ref[:, :D_MODEL]))))
    pl.when(j == 9)(lambda: per_chunk(store(lambda z: jax.nn.sigmoid(z + bgate_ref[:, D_MODEL:]))))


def _inproj(x, g_mix, w_in_bf, g_sgu, b_gate, cos_t, sin_t, tm, want_vnorm):
    n = x.shape[0]
    n_pos_blocks = cos_t.shape[0] // tm
    const = lambda i, j: (0, 0)
    n_out = 2 if want_vnorm else 1
    return pl.pallas_call(
        _inproj_kernel,
        grid=(n // tm, IN_COLS // D_MODEL),
        in_specs=[
            pl.BlockSpec((tm, D_MODEL), lambda i, j: (i, 0)),
            pl.BlockSpec((1, D_MODEL), const),
            pl.BlockSpec((D_MODEL, D_MODEL), lambda i, j: (0, j)),
            pl.BlockSpec((1, D_MODEL), const),
            pl.BlockSpec((1, 2 * D_MODEL), const),
            pl.BlockSpec((tm, B_DK // 2), lambda i, j: (i % n_pos_blocks, 0)),
            pl.BlockSpec((tm, B_DK // 2), lambda i, j: (i % n_pos_blocks, 0)),
        ],
        out_specs=[
            pl.BlockSpec((tm, D_MODEL), lambda i, j: (i, j)),
            pl.BlockSpec((tm, D_MODEL), lambda i, j: (i, 0)),
        ][:n_out],
        out_shape=[
            jax.ShapeDtypeStruct((n, IN_COLS), BF16),
            jax.ShapeDtypeStruct((n, D_MODEL), F32),
        ][:n_out],
        scratch_shapes=[pltpu.VMEM((tm, D_MODEL), BF16)],
        compiler_params=_cparams("parallel", "arbitrary"),
        name="inproj",
    )(x, g_mix.reshape(1, -1), w_in_bf, g_sgu.reshape(1, -1), b_gate.reshape(1, -1), cos_t, sin_t)


def _sgu_kernel(ac, u_ref, v_ref, wt_ref, bias_ref, wp_ref, pa_ref, wm_scr):
    r = u_ref.shape[0]

    @pl.when(pl.program_id(0) == 0)
    def _():
        row = lax.broadcasted_iota(jnp.int32, (r, r), 0)
        col = lax.broadcasted_iota(jnp.int32, (r, r), 1)
        keep = (row // ac == col // ac) & (col <= row)
        for g in range(A_GROUPS):
            wm_scr[g] = jnp.where(keep, wt_ref[g], 0.0).astype(BF16)

    v = v_ref[...]
    mixed = jnp.concatenate(
        [jnp.dot(wm_scr[g], v[:, g * A_GDIM:(g + 1) * A_GDIM], preferred_element_type=F32)
         for g in range(A_GROUPS)], axis=1) + bias_ref[...]
    y = (u_ref[...].astype(F32) * mixed).astype(BF16)
    pa_ref[...] = jnp.dot(y, wp_ref[...], preferred_element_type=F32).astype(BF16)


def _sgu(z_act, w_tiled, bias_full, w_proj_a_bf, ac, r):
    n = z_act.shape[0]
    return pl.pallas_call(
        functools.partial(_sgu_kernel, ac),
        grid=(n // r,),
        in_specs=[
            pl.BlockSpec((r, D_MODEL), lambda i: (i, 0)),
            pl.BlockSpec((r, D_MODEL), lambda i: (i, 1)),
            pl.BlockSpec((A_GROUPS, r, r), lambda i: (0, 0, 0)),
            pl.BlockSpec((r, D_MODEL), lambda i: (0, 0)),
            pl.BlockSpec((D_MODEL, D_MODEL), lambda i: (0, 0)),
        ],
        out_specs=pl.BlockSpec((r, D_MODEL), lambda i: (i, 0)),
        out_shape=jax.ShapeDtypeStruct((n, D_MODEL), BF16),
        scratch_shapes=[pltpu.VMEM((A_GROUPS, r, r), BF16)],
        compiler_params=_cparams("arbitrary"),
        name="sgu",
    )(z_act, z_act, w_tiled, bias_full, w_proj_a_bf)


def _ret_kernel(q_ref, k_ref, v_ref, g_ref, s0_ref, dmask_ref, qdec_ref, kdec_ref, gc_ref,
                wp_ref, pb_ref, sout_ref, s_scr):
    c = pl.program_id(1)

    @pl.when(c == 0)
    def _():
        s_scr[...] = s0_ref[0, 0]

    acc = None
    for h in range(B_HEADS):
        q = q_ref[:, h * B_DK:(h + 1) * B_DK]
        k = k_ref[:, h * B_DK:(h + 1) * B_DK]
        v = v_ref[:, h * B_DV:(h + 1) * B_DV]
        s = s_scr[h]
        scores = lax.dot_general(q, k, (((1,), (1,)), ((), ())), preferred_element_type=F32) * dmask_ref[h]
        o = jnp.dot(scores.astype(BF16), v, preferred_element_type=F32)
        qd = (q.astype(F32) * qdec_ref[h]).astype(BF16)
        o = o + jnp.dot(qd, s.astype(BF16), preferred_element_type=F32)
        kd = (k.astype(F32) * kdec_ref[h]).astype(BF16)
        s_scr[h] = gc_ref[h] * s + lax.dot_general(kd, v, (((0,), (0,)), ((), ())), preferred_element_type=F32)
        y = (g_ref[:, h * B_DV:(h + 1) * B_DV].astype(F32) * _rms(o)).astype(BF16)
        part = jnp.dot(y, wp_ref[h * B_DV:(h + 1) * B_DV, :], preferred_element_type=F32)
        acc = part if acc is None else acc + part
    pb_ref[...] = acc.astype(BF16)

    @pl.when(c == pl.num_programs(1) - 1)
    def _():
        sout_ref[0, 0] = s_scr[...]


def _ret(z_act, s0, w_proj_b_bf, n_seq, seq_len, chunk):
    n = z_act.shape[0]
    nc = seq_len // chunk
    dmask, qdec, kdec, gc = _ret_consts(chunk)
    row = lambda b, c: b * nc + c
    state_spec = pl.BlockSpec((1, 1, B_HEADS, B_DK, B_DV), lambda b, c: (0, b, 0, 0, 0))
    full = lambda shape: pl.BlockSpec(shape, lambda b, c: (0,) * len(shape))
    return pl.pallas_call(
        _ret_kernel,
        grid=(n_seq, nc),
        in_specs=[
            pl.BlockSpec((chunk, B_QK), lambda b, c: (row(b, c), 2)),
            pl.BlockSpec((chunk, B_QK), lambda b, c: (row(b, c), 3)),
            pl.BlockSpec((chunk, B_V), lambda b, c: (row(b, c), 2)),
            pl.BlockSpec((chunk, B_V), lambda b, c: (row(b, c), 3)),
            state_spec,
            full((B_HEADS, chunk, chunk)),
            full((B_HEADS, chunk, 1)),
            full((B_HEADS, chunk, 1)),
            full((B_HEADS, 1, 1)),
            full((B_V, D_MODEL)),
        ],
        out_specs=[
            pl.BlockSpec((chunk, D_MODEL), lambda b, c: (row(b, c), 0)),
            state_spec,
        ],
        out_shape=[
            jax.ShapeDtypeStruct((n, D_MODEL), BF16),
            jax.ShapeDtypeStruct((1, n_seq, B_HEADS, B_DK, B_DV), F32),
        ],
        scratch_shapes=[pltpu.VMEM((B_HEADS, B_DK, B_DV), F32)],
        compiler_params=_cparams("parallel", "arbitrary"),
        name="ret",
    )(z_act, z_act, z_act, z_act, s0, dmask, qdec, kdec, gc, w_proj_b_bf)


def _ret_consts(chunk):
    log_g = jnp.log(1.0 - 2.0 ** (-5.0 - jnp.arange(B_HEADS, dtype=F32)))
    idx = jnp.arange(chunk, dtype=F32)
    diff = idx[:, None] - idx[None, :]
    lg = log_g[:, None]
    dmask = jnp.where(diff[None] >= 0.0, jnp.exp(lg[:, :, None] * jnp.maximum(diff, 0.0)[None]), 0.0)
    qdec = jnp.exp(lg * (idx + 1.0))[:, :, None]
    kdec = jnp.exp(lg * (chunk - 1.0 - idx))[:, :, None]
    gc = jnp.exp(log_g * chunk)[:, None, None]
    return dmask, qdec, kdec, gc


def _merge_kernel(ga_ref, gb_ref, pa_ref, pb_ref, x_ref, wo_ref, gffn_ref, wq_ref, k1_ref, k2_ref,
                  x1_ref, t_ref, s1_ref, s2_ref):
    nt = (((1,), (1,)), ((), ()))
    for c in range(x_ref.shape[0] // MERGE_ROW_CHUNK):
        rows = slice(c * MERGE_ROW_CHUNK, (c + 1) * MERGE_ROW_CHUNK)
        m = (ga_ref[rows, :].astype(F32) * pa_ref[rows, :].astype(F32)
             + gb_ref[rows, :].astype(F32) * pb_ref[rows, :].astype(F32))
        x1 = x_ref[rows, :] + jnp.dot(m.astype(BF16), wo_ref[...], preferred_element_type=F32)
        x1_ref[rows, :] = x1
        t = _rms(x1) * gffn_ref[...]
        t_ref[rows, :] = t
        qq = jnp.dot(t.astype(BF16), wq_ref[...], preferred_element_type=F32).astype(BF16)
        for h in range(P_HEADS):
            base = h * 2 * P_HALF
            s1_ref[h, :, rows] = lax.dot_general(k1_ref[...], qq[:, base:base + P_HALF], nt,
                                                 preferred_element_type=F32)
            s2_ref[h, :, rows] = lax.dot_general(k2_ref[...], qq[:, base + P_HALF:base + 2 * P_HALF], nt,
                                                 preferred_element_type=F32)


def _merge(z_act, pa, pb, x, w_out_bf, g_ffn, w_query_bf, k1_bf, k2_bf, tm):
    n = x.shape[0]
    tok = lambda i: (i, 0)
    const = lambda i: (0, 0)
    return pl.pallas_call(
        _merge_kernel,
        grid=(n // tm,),
        in_specs=[
            pl.BlockSpec((tm, D_MODEL), lambda i: (i, 8)),
            pl.BlockSpec((tm, D_MODEL), lambda i: (i, 9)),
            pl.BlockSpec((tm, D_MODEL), tok),
            pl.BlockSpec((tm, D_MODEL), tok),
            pl.BlockSpec((tm, D_MODEL), tok),
            pl.BlockSpec((D_MODEL, D_MODEL), const),
            pl.BlockSpec((1, D_MODEL), const),
            pl.BlockSpec((D_MODEL, 2 * P_HALF * P_HEADS), const),
            pl.BlockSpec((P_NKEYS, P_HALF), const),
            pl.BlockSpec((P_NKEYS, P_HALF), const),
        ],
        out_specs=[
            pl.BlockSpec((tm, D_MODEL), tok),
            pl.BlockSpec((tm, D_MODEL), tok),
            pl.BlockSpec((P_HEADS, P_NKEYS, tm), lambda i: (0, 0, i)),
            pl.BlockSpec((P_HEADS, P_NKEYS, tm), lambda i: (0, 0, i)),
        ],
        out_shape=[
            jax.ShapeDtypeStruct((n, D_MODEL), F32),
            jax.ShapeDtypeStruct((n, D_MODEL), F32),
            jax.ShapeDtypeStruct((P_HEADS, P_NKEYS, n), F32),
            jax.ShapeDtypeStruct((P_HEADS, P_NKEYS, n), F32),
        ],
        compiler_params=_cparams("parallel"),
        name="merge",
    )(z_act, z_act, pa, pb, x, w_out_bf, g_ffn.reshape(1, -1), w_query_bf, k1_bf, k2_bf)


def _sort_network(n):
    pairs = []
    p = 1
    while p < n:
        k = p
        while k >= 1:
            for j in range(k % p, n - k, 2 * k):
                for i in range(min(k, n - j - k)):
                    if (i + j) // (2 * p) == (i + j + k) // (2 * p):
                        pairs.append((i + j, i + j + k))
            k //= 2
        p *= 2
    return pairs


def _top_of_keys(scores, count):
    tl = scores.shape[1]
    n_tiles = P_NKEYS // 8
    sub = lax.broadcasted_iota(jnp.int32, (8, tl), 0).astype(F32)
    vals = [scores[8 * i:8 * (i + 1)] for i in range(n_tiles)]
    ids = [sub + float(8 * i) for i in range(n_tiles)]
    for i, j in _sort_network(n_tiles):
        swap = (vals[j] > vals[i]) | ((vals[j] == vals[i]) & (ids[j] < ids[i]))
        vals[i], vals[j] = jnp.where(swap, vals[j], vals[i]), jnp.where(swap, vals[i], vals[j])
        ids[i], ids[j] = jnp.where(swap, ids[j], ids[i]), jnp.where(swap, ids[i], ids[j])
    out_v, out_i = [], []
    for it in range(count):
        m = jnp.max(vals[0], axis=0, keepdims=True)
        sel = jnp.min(jnp.where(vals[0] == m, ids[0], float(P_NKEYS)), axis=0, keepdims=True)
        out_v.append(m)
        out_i.append(sel)
        hit = ids[0] == sel
        for d in range(count - 1 - it):
            vals[d] = jnp.where(hit, vals[d + 1], vals[d])
            ids[d] = jnp.where(hit, ids[d + 1], ids[d])
    return out_v, out_i


def _candidate_layout():
    pairs = [(a, b) for a in range(P_TOPK) for b in range(P_TOPK // (a + 1))]
    pairs += [None] * (-len(pairs) % 8)
    return [pairs[i:i + 8] for i in range(0, len(pairs), 8)]


def _topk_kernel(s1_ref, s2_ref, idx_ref, w_ref, idx_scr, w_scr):
    tl = s1_ref.shape[2]
    layout = _candidate_layout()
    big = float(P_TOPK * P_TOPK)
    row = lambda x: jnp.full((1, tl), x, F32)
    cf = jnp.concatenate([row(big if p is None else p[0] * P_TOPK + p[1]) for tile in layout for p in tile], axis=0)

    def head_body(h, carry):
        v1, i1 = _top_of_keys(s1_ref[h], P_TOPK)
        v2, i2 = _top_of_keys(s2_ref[h], P_TOPK)
        cv = jnp.concatenate([row(-jnp.inf) if p is None else v1[p[0]] + v2[p[1]]
                              for tile in layout for p in tile], axis=0)
        ce = jnp.concatenate([row(-1.0) if p is None else i1[p[0]] * P_NKEYS + i2[p[1]]
                              for tile in layout for p in tile], axis=0)
        vals, experts = [], []
        for _ in range(P_TOPK):
            m = jnp.max(cv, axis=0, keepdims=True)
            sel = jnp.min(jnp.where(cv == m, cf, big), axis=0, keepdims=True)
            hit = cf == sel
            vals.append(m)
            experts.append(jnp.max(jnp.where(hit, ce, -1.0), axis=0, keepdims=True))
            cv = jnp.where(hit, -jnp.inf, cv)
        vals = jnp.concatenate(vals, axis=0)
        e = jnp.exp(vals - vals[0:1])
        w = e / jnp.sum(e, axis=0, keepdims=True)
        rows = pl.ds(pl.multiple_of(h * P_TOPK, P_TOPK), P_TOPK)
        idx_scr[rows, :] = jnp.concatenate(experts, axis=0).astype(jnp.int32) * TABLE_ROWS_PER_EXPERT
        w_scr[rows, :] = w
        return carry

    def head_pair(g, carry):
        head_body(2 * g, carry)
        head_body(2 * g + 1, carry)
        return carry

    lax.fori_loop(0, P_HEADS // 2, head_pair, 0)
    idx_ref[...] = idx_scr[...].T
    w_ref[...] = w_scr[...].T


def _topk(s1t, s2t, tl):
    n = s1t.shape[2]
    sc = pl.BlockSpec((P_HEADS, P_NKEYS, tl), lambda i: (0, 0, i))
    tok = pl.BlockSpec((tl, P_PAIRS), lambda i: (i, 0))
    return pl.pallas_call(
        _topk_kernel,
        grid=(n // tl,),
        in_specs=[sc, sc],
        out_specs=[tok, tok],
        out_shape=[
            jax.ShapeDtypeStruct((n, P_PAIRS), jnp.int32),
            jax.ShapeDtypeStruct((n, P_PAIRS), F32),
        ],
        scratch_shapes=[pltpu.VMEM((P_PAIRS, tl), jnp.int32), pltpu.VMEM((P_PAIRS, tl), F32)],
        compiler_params=_cparams("parallel"),
        name="topk",
    )(s1t, s2t)


def _unpack_rows(x):
    lo = lax.bitcast_convert_type(x << 16, F32)
    hi = lax.bitcast_convert_type(x & jnp.uint32(0xFFFF0000), F32)
    return lo, hi


def _copy_rows(tab_ref, idx_ref, tokens, dst_refs):
    rpe = TABLE_ROWS_PER_EXPERT
    picks = [idx_ref.at[t] for t in tokens]
    for k in range(P_PAIRS):
        for pick_row, dst_ref in zip(picks, dst_refs):
            row = pl.multiple_of(pick_row[k], rpe)
            dst_ref[k * rpe:(k + 1) * rpe, :] = tab_ref[pl.ds(row, rpe), :]


def _pipelined_groups(tb, idx_ref, tab_ref, g_scrs, compute_token):
    grp = len(g_scrs) // 2
    n_groups = tb // grp
    bufs = (g_scrs[:grp], g_scrs[grp:])

    def copy_group(g, which):
        _copy_rows(tab_ref, idx_ref, [g * grp + u for u in range(grp)], bufs[which])

    def compute_group(g, which):
        for u in range(grp):
            compute_token(g * grp + u, bufs[which][u])

    copy_group(0, 0)

    def pair_body(p, carry):
        g = 2 * p
        copy_group(g + 1, 1)
        compute_group(g, 0)
        copy_group(g + 2, 0)
        compute_group(g + 1, 1)
        return carry

    lax.fori_loop(0, n_groups // 2 - 1, pair_body, 0)
    copy_group(n_groups - 1, 1)
    compute_group(n_groups - 2, 0)
    compute_group(n_groups - 1, 1)


def _peer_u_kernel(idx_ref, t_ref, w_ref, tab_ref, c_ref, *g_scrs):
    tb = t_ref.shape[0]
    rpe = TABLE_ROWS_PER_EXPERT
    ones = jnp.ones((8, 2 * LANES), BF16)
    contract_lanes = (((1,), (1,)), ((), ()))

    def compute_token(t, g_scr):
        t_row = t_ref[pl.ds(t, 1), :]
        part = None
        for r in range(rpe):
            lo, hi = _unpack_rows(g_scr[pl.ds(r, P_PAIRS, stride=rpe), :])
            t_lo = t_row[:, r * LANES:(r + 1) * LANES]
            t_hi = t_row[:, (r + rpe) * LANES:(r + rpe + 1) * LANES]
            term = lo * t_lo + hi * t_hi
            part = term if part is None else part + term
        p_hi = part.astype(BF16)
        p_lo = (part - p_hi.astype(F32)).astype(BF16)
        sums = lax.dot_general(ones, jnp.concatenate([p_hi, p_lo], axis=1), contract_lanes,
                               preferred_element_type=F32)
        c_ref[pl.ds(t, 1), :] = sums[0:1]

    _pipelined_groups(tb, idx_ref, tab_ref, g_scrs, compute_token)
    c_ref[...] = w_ref[...] * jax.nn.gelu(c_ref[...])


def _peer_u(idx_t, t, w_t, table, tb):
    n = t.shape[0]
    return pl.pallas_call(
        _peer_u_kernel,
        grid=(n // tb,),
        in_specs=[
            pl.BlockSpec((tb, P_PAIRS), lambda i: (i, 0), memory_space=pltpu.SMEM),
            pl.BlockSpec((tb, D_MODEL), lambda i: (i, 0)),
            pl.BlockSpec((tb, P_PAIRS), lambda i: (i, 0)),
            pl.BlockSpec(table.shape, lambda i: (0, 0), pipeline_mode=pl.Buffered(1)),
        ],
        out_specs=pl.BlockSpec((tb, P_PAIRS), lambda i: (i, 0)),
        out_shape=jax.ShapeDtypeStruct((n, P_PAIRS), F32),
        scratch_shapes=[pltpu.VMEM((P_PAIRS * TABLE_ROWS_PER_EXPERT, LANES), jnp.uint32)] * (2 * PEER_TOKENS_PER_ITER),
        compiler_params=_cparams("arbitrary"),
        name="peer_u",
    )(idx_t, t, w_t, table)


def _peer_v_kernel(idx_ref, c_ref, x1_ref, gfin_ref, tab_ref, y_ref, *g_scrs):
    tb = x1_ref.shape[0]
    rpe = TABLE_ROWS_PER_EXPERT

    def compute_token(t, g_scr):
        cmat = jnp.broadcast_to(c_ref[pl.ds(t, 1), :], (P_PAIRS, P_PAIRS)).T
        lo_rows, hi_rows = [], []
        for r in range(rpe):
            lo, hi = _unpack_rows(g_scr[pl.ds(r, P_PAIRS, stride=rpe), :])
            lo_rows.append(jnp.sum(lo * cmat, axis=0, keepdims=True))
            hi_rows.append(jnp.sum(hi * cmat, axis=0, keepdims=True))
        y_ref[pl.ds(t, 1), :] = x1_ref[pl.ds(t, 1), :] + jnp.concatenate(lo_rows + hi_rows, axis=1)

    _pipelined_groups(tb, idx_ref, tab_ref, g_scrs, compute_token)
    y_ref[...] = _rms(y_ref[...]) * gfin_ref[...]


def _peer_v(idx_t, c_t, x1, g_final, table, tb):
    n = x1.shape[0]
    smem = pl.BlockSpec((tb, P_PAIRS), lambda i: (i, 0), memory_space=pltpu.SMEM)
    tok = pl.BlockSpec((tb, D_MODEL), lambda i: (i, 0))
    return pl.pallas_call(
        _peer_v_kernel,
        grid=(n // tb,),
        in_specs=[
            smem, pl.BlockSpec((tb, P_PAIRS), lambda i: (i, 0)), tok,
            pl.BlockSpec((1, D_MODEL), lambda i: (0, 0)),
            pl.BlockSpec(table.shape, lambda i: (0, 0), pipeline_mode=pl.Buffered(1)),
        ],
        out_specs=tok,
        out_shape=jax.ShapeDtypeStruct((n, D_MODEL), F32),
        scratch_shapes=[pltpu.VMEM((P_PAIRS * TABLE_ROWS_PER_EXPERT, LANES), jnp.uint32)] * (2 * PEER_TOKENS_PER_ITER),
        compiler_params=_cparams("arbitrary"),
        name="peer_v",
    )(idx_t, c_t, x1, g_final.reshape(1, D_MODEL), table)


def _pack_kernel(e_ref, o_ref):
    rpe = TABLE_ROWS_PER_EXPERT
    n_exp = e_ref.shape[0]

    def bf16_bits_high(x):
        return lax.bitcast_convert_type(x.astype(BF16).astype(F32), jnp.uint32)

    for r in range(rpe):
        lo = bf16_bits_high(e_ref[:, r * LANES:(r + 1) * LANES]) >> 16
        hi = bf16_bits_high(e_ref[:, (r + rpe) * LANES:(r + rpe + 1) * LANES])
        o_ref[pl.ds(r, n_exp, stride=rpe), :] = hi | lo


def _pack_table(e):
    n_exp = e.shape[0]
    blk = 512
    assert n_exp % blk == 0
    return pl.pallas_call(
        _pack_kernel,
        grid=(n_exp // blk,),
        in_specs=[pl.BlockSpec((blk, D_MODEL), lambda i: (i, 0))],
        out_specs=pl.BlockSpec((blk * TABLE_ROWS_PER_EXPERT, LANES), lambda i: (i, 0)),
        out_shape=jax.ShapeDtypeStruct((n_exp * TABLE_ROWS_PER_EXPERT, LANES), jnp.uint32),
        compiler_params=_cparams("parallel"),
        name="pack_table",
    )(e)


def _rope_tables(pos):
    half = B_DK // 2
    inv = 1.0 / (ROPE_BASE ** jnp.linspace(0.0, 1.0, half, dtype=F32))
    ang = pos.astype(F32)[:, None] * inv[None, :]
    return jnp.cos(ang), jnp.sin(ang)


def _layer(x, n_seq, seq_len, pos0, s0, p, tabs, g_final, want_vnorm):
    n = x.shape[0]
    tm_in = min(n, 2048)
    rows = 256
    ac = min(seq_len, A_CHUNK)
    chunk = min(seq_len, 256)
    assert seq_len % chunk == 0 and rows % ac == 0 and n % rows == 0 and n % tm_in == 0
    assert seq_len <= RET_CHUNK or seq_len % RET_CHUNK == 0

    pos = pos0 + jnp.arange(seq_len)
    if seq_len < tm_in:
        pos = jnp.tile(pos, tm_in // seq_len)
    cos_t, sin_t = _rope_tables(pos)
    z_act, *v_norm = _inproj(x, p["g_mix"], p["w_in"], p["g_sgu"], p["b_gate"], cos_t, sin_t, tm_in, want_vnorm)

    w_tiled = jnp.tile(p["w_s"][:, :ac, :ac], (1, rows // ac, rows // ac))
    bias_full = jnp.repeat(jnp.tile(p["b_s"][:, :ac].T, (rows // ac, 1)), A_GDIM, axis=1)
    pa = _sgu(z_act, w_tiled, bias_full, p["w_proj_a"], ac, rows)
    pb, s_new = _ret(z_act, s0, p["w_proj_b"], n_seq, seq_len, chunk)

    x1, t, s1t, s2t = _merge(z_act, pa, pb, x, p["w_out"], p["g_ffn"], p["w_query"], p["k1"], p["k2"], rows)
    idx_t, w_t = _topk(s1t, s2t, LANES)
    c_t = _peer_u(idx_t, t, w_t, tabs[0], LANES)
    y = _peer_v(idx_t, c_t, x1, g_final, tabs[1], LANES)
    return y, s_new, (v_norm[0] if want_vnorm else None)


def kernel(x_prompt, x_sample, state_ret, w_in, w_s, b_s, g_sgu, w_proj_a, w_proj_b, b_gate, w_out, g_mix,
           g_ffn, w_query, sub_keys_1, sub_keys_2, expert_u, expert_v, g_final):
    assert w_in.shape[0] == 1, "single-layer trunk only"
    bp, lp, _ = x_prompt.shape
    bs, ls, _ = x_sample.shape
    past_len = 1024
    p = dict(
        g_mix=g_mix[0], w_in=w_in[0].astype(BF16), g_sgu=g_sgu[0], b_gate=b_gate[0],
        w_s=w_s[0], b_s=b_s[0], w_proj_a=w_proj_a[0].astype(BF16), w_proj_b=w_proj_b[0].astype(BF16),
        w_out=w_out[0].astype(BF16), g_ffn=g_ffn[0], w_query=w_query[0].astype(BF16),
        k1=sub_keys_1[0].astype(BF16), k2=sub_keys_2[0].astype(BF16))
    tabs = (_pack_table(expert_u[0]), _pack_table(expert_v[0]))
    s0p = jnp.zeros((1, bp, B_HEADS, B_DK, B_DV), F32)
    yp, sp, _ = _layer(x_prompt.reshape(bp * lp, D_MODEL), bp, lp, 0, s0p, p, tabs, g_final, False)
    ys, ss, vs = _layer(x_sample.reshape(bs * ls, D_MODEL), bs, ls, past_len, state_ret, p, tabs, g_final, True)
    return (yp.reshape(bp, lp, D_MODEL), ys.reshape(bs, ls, D_MODEL), sp, ss, vs.reshape(1, bs, ls, D_MODEL))
```
